```python
import jax, jax.numpy as jnp
from jax import lax
import numpy as np

D_MODEL = 2048
BATCH = 4
SEQ = 2048
DEPTH = 1

HEAD_DIM = 128
N_Q_HEADS = 8
N_KV_GROUPS = 2
HEADS_PER_GROUP = N_Q_HEADS // N_KV_GROUPS
NSA_WIDTH = N_Q_HEADS * HEAD_DIM
KV_WIDTH = N_KV_GROUPS * HEAD_DIM
N_BRANCH = 3
N_GATES = N_BRANCH * N_Q_HEADS
CONV_WIDTH = D_MODEL - NSA_WIDTH
MIX_WIDTH = NSA_WIDTH + CONV_WIDTH
IN_SPLITS = (NSA_WIDTH,) + (KV_WIDTH,) * 6 + (N_GATES, 2 * CONV_WIDTH)
IN_WIDTH = sum(IN_SPLITS)

CMP_BLOCK = 32
CMP_STRIDE = 16
SEL_BLOCK = 64
SEL_TOPK = 16
SEL_QCHUNK = 64
WINDOW = 512
WIN_QBLOCK = 128
ROPE_THETA = 10000.0

CONV_KERNEL = 31

N_EXPERTS = 256
TOP_K = 8
N_EXPERT_GROUPS = 8
TOPK_GROUPS = 4
EXPERT_HIDDEN = 512
SHARED_HIDDEN = 512
ROUTE_SCALE = 2.5
DISPATCH_BLOCK = 128

EPS = 1e-6
NEG_INF = -1e30
FORCE = 1e30

kernel_name = 'hymba_nsa_conformer_moe_block'


def rms_norm(x, g):
    xf = x.astype(jnp.float32)
    y = xf * lax.rsqrt(jnp.mean(xf * xf, axis=-1, keepdims=True) + EPS)
    return (y * g.astype(jnp.float32)).astype(x.dtype)


def rope(t, pos):
    half = t.shape[-1] // 2
    inv = ROPE_THETA ** (-jnp.arange(half, dtype=jnp.float32) / half)
    ang = pos.astype(jnp.float32)[..., None] * inv
    cos = jnp.cos(ang)[:, :, None, :]
    sin = jnp.sin(ang)[:, :, None, :]
    tf = t.astype(jnp.float32)
    t1, t2 = tf[..., :half], tf[..., half:]
    return jnp.concatenate([t1 * cos - t2 * sin, t1 * sin + t2 * cos], axis=-1).astype(t.dtype)


def compress_kv(kv, pe, w1, w2):
    B, S, G, d = kv.shape
    n_sub = CMP_BLOCK // CMP_STRIDE
    chunks = kv.reshape(B, S // CMP_STRIDE, CMP_STRIDE, G, d)
    nc = S // CMP_STRIDE - n_sub + 1
    blocks = jnp.concatenate([chunks[:, j:j + nc] for j in range(n_sub)], axis=2)
    blocks = blocks + pe[None, None, :, None, :]
    flat = blocks.transpose(0, 1, 3, 2, 4).reshape(B, nc, G, CMP_BLOCK * d)
    return jax.nn.silu(flat @ w1) @ w2


def nsa_attention(q, k_cmp, v_cmp, k_sel, v_sel, k_win, v_win, gates, positions,
                  cmp_k_pe, cmp_k_w1, cmp_k_w2, cmp_v_pe, cmp_v_w1, cmp_v_w2):
    B, S = q.shape[:2]
    G, HPG, d = N_KV_GROUPS, HEADS_PER_GROUP, HEAD_DIM
    scale = d ** -0.5
    t = jnp.arange(S)
    q = rope(q, positions)
    qg = q.reshape(B, S, G, HPG, d).transpose(0, 2, 3, 1, 4)

    k_c = rope(compress_kv(k_cmp, cmp_k_pe, cmp_k_w1, cmp_k_w2),
               positions[:, CMP_BLOCK - 1::CMP_STRIDE])
    v_c = compress_kv(v_cmp, cmp_v_pe, cmp_v_w1, cmp_v_w2)
    nc = k_c.shape[1]
    blk_end = jnp.arange(nc) * CMP_STRIDE + CMP_BLOCK - 1
    valid_c = blk_end[None, :] <= t[:, None]
    s_c = jnp.einsum('bghtd,bcgd->bghtc', qg, k_c).astype(jnp.float32) * scale
    s_c = jnp.where(valid_c, s_c, NEG_INF)
    any_c = (t >= CMP_BLOCK - 1).astype(jnp.float32)
    p_c = jax.nn.softmax(s_c, axis=-1) * any_c[:, None]
    o_c = jnp.einsum('bghtc,bcgd->bghtd', p_c.astype(v_c.dtype), v_c)

    n_sel = S // SEL_BLOCK
    cs = np.arange(nc) * CMP_STRIDE
    js = np.arange(n_sel) * SEL_BLOCK
    overlap = ((cs[:, None] < js[None, :] + SEL_BLOCK) &
               (cs[:, None] + CMP_BLOCK > js[None, :])).astype(np.float32)
    imp = jnp.einsum('bgtc,cj->bgtj', p_c.sum(axis=2), jnp.asarray(overlap))
    jb = jnp.arange(n_sel)
    cur = (t // SEL_BLOCK)[:, None]
    forced = (jb[None, :] == 0) | (jb[None, :] == cur) | (jb[None, :] == cur - 1)
    causal_blk = jb[None, :] * SEL_BLOCK <= t[:, None]
    imp = jnp.where(forced, FORCE, jnp.where(causal_blk, imp, NEG_INF))
    k_top = min(SEL_TOPK, n_sel)
    _, sel_idx = lax.top_k(imp, k_top)

    ks_r = rope(k_sel, positions)
    Kb = ks_r.reshape(B, n_sel, SEL_BLOCK, G, d).transpose(0, 3, 1, 2, 4)
    Vb = v_sel.reshape(B, n_sel, SEL_BLOCK, G, d).transpose(0, 3, 1, 2, 4)
    n_ch = S // SEL_QCHUNK
    q_ch = qg.reshape(B, G, HPG, n_ch, SEL_QCHUNK, d).transpose(3, 0, 1, 2, 4, 5)
    idx_ch = sel_idx.reshape(B, G, n_ch, SEL_QCHUNK, k_top).transpose(2, 0, 1, 3, 4)
    t_ch = t.reshape(n_ch, SEL_QCHUNK)
    bi = jnp.arange(B)[:, None, None, None]
    gi = jnp.arange(G)[None, :, None, None]
    offs = jnp.arange(SEL_BLOCK)

    def sel_block(args):
        qc, ic, tc = args
        kg = Kb[bi, gi, ic]
        vg = Vb[bi, gi, ic]
        s = jnp.einsum('bghtd,bgtkld->bghtkl', qc, kg).astype(jnp.float32) * scale
        kpos = ic[..., None] * SEL_BLOCK + offs
        m = kpos <= tc[None, None, :, None, None]
        s = jnp.where(m[:, :, None], s, NEG_INF)
        p = jax.nn.softmax(s.reshape(s.shape[:4] + (-1,)), axis=-1).reshape(s.shape)
        return jnp.einsum('bghtkl,bgtkld->bghtd', p.astype(vg.dtype), vg)

    o_s = lax.map(sel_block, (q_ch, idx_ch, t_ch))
    o_s = o_s.transpose(1, 2, 3, 0, 4, 5).reshape(B, G, HPG, S, d)

    nqb = S // WIN_QBLOCK
    span = WINDOW + WIN_QBLOCK
    kpad = jnp.pad(rope(k_win, positions), ((0, 0), (WINDOW, 0), (0, 0), (0, 0)))
    vpad = jnp.pad(v_win, ((0, 0), (WINDOW, 0), (0, 0), (0, 0)))
    win_idx = (jnp.arange(nqb) * WIN_QBLOCK)[:, None] + jnp.arange(span)[None, :]
    kw = kpad[:, win_idx]
    vw = vpad[:, win_idx]
    qw = qg.reshape(B, G, HPG, nqb, WIN_QBLOCK, d)
    s_w = jnp.einsum('bghnid,bnjgd->bghnij', qw, kw).astype(jnp.float32) * scale
    tq = t.reshape(nqb, WIN_QBLOCK)[:, :, None]
    kp = (win_idx - WINDOW)[:, None, :]
    m_w = (kp >= 0) & (kp <= tq) & (tq - kp < WINDOW)
    p_w = jax.nn.softmax(jnp.where(m_w, s_w, NEG_INF), axis=-1)
    o_w = jnp.einsum('bghnij,bnjgd->bghnid', p_w.astype(vw.dtype), vw).reshape(B, G, HPG, S, d)

    g = jax.nn.sigmoid(gates.astype(jnp.float32)).astype(q.dtype)
    g = g.reshape(B, S, N_BRANCH, G, HPG).transpose(2, 0, 3, 4, 1)[..., None]
    o = g[0] * o_c + g[1] * o_s + g[2] * o_w
    return o.transpose(0, 3, 1, 2, 4).reshape(B, S, NSA_WIDTH)


def conformer_conv(u, conv_w, conv_b, ln_g, ln_b):
    a, b = jnp.split(u, 2, axis=-1)
    y = a * jax.nn.sigmoid(b)
    y = lax.conv_general_dilated(y, conv_w, window_strides=(1,),
                                 padding=[(CONV_KERNEL - 1, 0)],
                                 dimension_numbers=('NWC', 'WIO', 'NWC'),
                                 feature_group_count=CONV_WIDTH) + conv_b
    yf = y.astype(jnp.float32)
    mu = jnp.mean(yf, axis=-1, keepdims=True)
    var = jnp.mean(jnp.square(yf - mu), axis=-1, keepdims=True)
    yn = (yf - mu) * lax.rsqrt(var + EPS) * ln_g.astype(jnp.float32) + ln_b.astype(jnp.float32)
    return jax.nn.silu(yn).astype(u.dtype)


def hybrid_mixer(h, positions, w_in, cmp_k_pe, cmp_k_w1, cmp_k_w2, cmp_v_pe, cmp_v_w1,
                 cmp_v_w2, conv_w, conv_b, conv_ln_g, conv_ln_b, w_out):
    B, S, _ = h.shape
    proj = h @ w_in
    split_pts = np.cumsum(IN_SPLITS)[:-1].tolist()
    q, kc, vc, ksl, vsl, kwn, vwn, gates, glu_in = jnp.split(proj, split_pts, axis=-1)
    qh = q.reshape(B, S, N_Q_HEADS, HEAD_DIM)
    kvh = lambda z: z.reshape(B, S, N_KV_GROUPS, HEAD_DIM)
    o_attn = nsa_attention(qh, kvh(kc), kvh(vc), kvh(ksl), kvh(vsl), kvh(kwn), kvh(vwn),
                           gates, positions, cmp_k_pe, cmp_k_w1, cmp_k_w2,
                           cmp_v_pe, cmp_v_w1, cmp_v_w2)
    o_conv = conformer_conv(glu_in, conv_w, conv_b, conv_ln_g, conv_ln_b)
    return jnp.concatenate([o_attn, o_conv], axis=-1) @ w_out


def route(h2, w_router, router_bias):
    T = h2.shape[0]
    s = jax.nn.sigmoid((h2 @ w_router).astype(jnp.float32))
    sb = s + router_bias.astype(jnp.float32)
    grp = sb.reshape(T, N_EXPERT_GROUPS, N_EXPERTS // N_EXPERT_GROUPS)
    grp_score = lax.top_k(grp, 2)[0].sum(-1)
    _, gidx = lax.top_k(grp_score, TOPK_GROUPS)
    gmask = jax.nn.one_hot(gidx, N_EXPERT_GROUPS).sum(-2) > 0
    emask = jnp.repeat(gmask, N_EXPERTS // N_EXPERT_GROUPS, axis=-1)
    _, top_idx = lax.top_k(jnp.where(emask, sb, NEG_INF), TOP_K)
    top_s = jnp.take_along_axis(s, top_idx, axis=-1)
    top_w = top_s / jnp.sum(top_s, axis=-1, keepdims=True) * ROUTE_SCALE
    return top_idx, top_w


def routed_experts(h2, top_idx, top_w, w_gate, w_up, w_down):
    T, D = h2.shape
    E = w_gate.shape[0]
    A = T * TOP_K
    e_flat = top_idx.reshape(A)
    tok_flat = jnp.arange(A, dtype=jnp.int32) // TOP_K
    w_flat = top_w.reshape(A)
    order = jnp.argsort(e_flat)
    e_sorted = e_flat[order]
    counts = jnp.bincount(e_flat, length=E)
    starts = jnp.cumsum(counts) - counts
    padded = (counts + DISPATCH_BLOCK - 1) // DISPATCH_BLOCK * DISPATCH_BLOCK
    pad_ends = jnp.cumsum(padded)
    pad_starts = pad_ends - padded
    dest = pad_starts[e_sorted] + (jnp.arange(A) - starts[e_sorted])
    n_blocks = -(-(A + E * (DISPATCH_BLOCK - 1)) // DISPATCH_BLOCK)
    P = n_blocks * DISPATCH_BLOCK
    row_tok = jnp.full((P,), T, jnp.int32).at[dest].set(tok_flat[order])
    row_w = jnp.zeros((P,), h2.dtype).at[dest].set(w_flat[order])
    block_e = jnp.minimum(jnp.searchsorted(pad_ends, jnp.arange(n_blocks) * DISPATCH_BLOCK,
                                           side='right'), E - 1)
    h_pad = jnp.concatenate([h2, jnp.zeros((1, D), h2.dtype)], axis=0)

    def block_fn(args):
        toks, wts, e = args
        xb = h_pad[toks]
        y = (jax.nn.silu(xb @ w_gate[e]) * (xb @ w_up[e])) @ w_down[e]
        return y * wts[:, None]

    y = lax.map(block_fn, (row_tok.reshape(n_blocks, DISPATCH_BLOCK),
                           row_w.reshape(n_blocks, DISPATCH_BLOCK), block_e))
    return jax.ops.segment_sum(y.reshape(P, D), row_tok, num_segments=T + 1)[:T]


def moe_ffn(h, w_router, router_bias, w_exp_gate, w_exp_up, w_exp_down,
            w_sh_gate, w_sh_up, w_sh_down):
    B, S, D = h.shape
    h2 = h.reshape(B * S, D)
    top_idx, top_w = route(h2, w_router, router_bias)
    routed = routed_experts(h2, top_idx, top_w.astype(h2.dtype), w_exp_gate, w_exp_up, w_exp_down)
    shared = (jax.nn.silu(h2 @ w_sh_gate) * (h2 @ w_sh_up)) @ w_sh_down
    return (routed + shared).reshape(B, S, D)


def setup_inputs(seed: int = 0) -> dict:
    key = jax.random.key(seed)
    ks = jax.random.split(key, 32)
    f32 = jnp.float32
    L = DEPTH

    def nrm(k, shape, sc):
        return jax.random.normal(k, shape, f32) * sc

    return {
        'x': nrm(ks[0], (BATCH, SEQ, D_MODEL), 1.0),
        'c': nrm(ks[1], (BATCH, D_MODEL), 1.0),
        'positions': jnp.tile(jnp.arange(SEQ, dtype=jnp.int32)[None, :], (BATCH, 1)),
        'w_ada': nrm(ks[2], (L, D_MODEL, 6 * D_MODEL), 0.5 * D_MODEL ** -0.5),
        'b_ada': nrm(ks[3], (L, 6 * D_MODEL), 0.01),
        'g_mix': 1.0 + nrm(ks[4], (L, D_MODEL), 0.02),
        'w_in': nrm(ks[5], (L, D_MODEL, IN_WIDTH), D_MODEL ** -0.5),
        'cmp_k_pe': nrm(ks[6], (L, CMP_BLOCK, HEAD_DIM), 0.1),
        'cmp_k_w1': nrm(ks[7], (L, CMP_BLOCK * HEAD_DIM, HEAD_DIM), (CMP_BLOCK * HEAD_DIM) ** -0.5),
        'cmp_k_w2': nrm(ks[8], (L, HEAD_DIM, HEAD_DIM), HEAD_DIM ** -0.5),
        'cmp_v_pe': nrm(ks[9], (L, CMP_BLOCK, HEAD_DIM), 0.1),
        'cmp_v_w1': nrm(ks[10], (L, CMP_BLOCK * HEAD_DIM, HEAD_DIM), (CMP_BLOCK * HEAD_DIM) ** -0.5),
        'cmp_v_w2': nrm(ks[11], (L, HEAD_DIM, HEAD_DIM), HEAD_DIM ** -0.5),
        'conv_w': nrm(ks[12], (L, CONV_KERNEL, 1, CONV_WIDTH), CONV_KERNEL ** -0.5),
        'conv_b': nrm(ks[13], (L, CONV_WIDTH), 0.01),
        'conv_ln_g': 1.0 + nrm(ks[14], (L, CONV_WIDTH), 0.02),
        'conv_ln_b': nrm(ks[15], (L, CONV_WIDTH), 0.01),
        'w_out': nrm(ks[16], (L, MIX_WIDTH, D_MODEL), MIX_WIDTH ** -0.5),
        'g_ffn': 1.0 + nrm(ks[17], (L, D_MODEL), 0.02),
        'w_router': nrm(ks[18], (L, D_MODEL, N_EXPERTS), D_MODEL ** -0.5),
        'router_bias': nrm(ks[19], (L, N_EXPERTS), 0.01),
        'w_exp_gate': nrm(ks[20], (L, N_EXPERTS, D_MODEL, EXPERT_HIDDEN), D_MODEL ** -0.5),
        'w_exp_up': nrm(ks[21], (L, N_EXPERTS, D_MODEL, EXPERT_HIDDEN), D_MODEL ** -0.5),
        'w_exp_down': nrm(ks[22], (L, N_EXPERTS, EXPERT_HIDDEN, D_MODEL), EXPERT_HIDDEN ** -0.5),
        'w_sh_gate': nrm(ks[23], (L, D_MODEL, SHARED_HIDDEN), D_MODEL ** -0.5),
        'w_sh_up': nrm(ks[24], (L, D_MODEL, SHARED_HIDDEN), D_MODEL ** -0.5),
        'w_sh_down': nrm(ks[25], (L, SHARED_HIDDEN, D_MODEL), SHARED_HIDDEN ** -0.5),
        'g_final': 1.0 + nrm(ks[26], (D_MODEL,), 0.02),
    }


def reference(x, c, positions, w_ada, b_ada, g_mix, w_in, cmp_k_pe, cmp_k_w1, cmp_k_w2,
              cmp_v_pe, cmp_v_w1, cmp_v_w2, conv_w, conv_b, conv_ln_g, conv_ln_b, w_out,
              g_ffn, w_router, router_bias, w_exp_gate, w_exp_up, w_exp_down,
              w_sh_gate, w_sh_up, w_sh_down, g_final):
    for l in range(DEPTH):
        mod = jax.nn.silu(c) @ w_ada[l] + b_ada[l]
        sh_m, sc_m, gt_m, sh_f, sc_f, gt_f = jnp.split(mod[:, None, :], 6, axis=-1)
        h = rms_norm(x, g_mix[l]) * (1.0 + sc_m) + sh_m
        x = x + gt_m * hybrid_mixer(h, positions, w_in[l], cmp_k_pe[l], cmp_k_w1[l],
                                    cmp_k_w2[l], cmp_v_pe[l], cmp_v_w1[l], cmp_v_w2[l],
                                    conv_w[l], conv_b[l], conv_ln_g[l], conv_ln_b[l], w_out[l])
        h = rms_norm(x, g_ffn[l]) * (1.0 + sc_f) + sh_f
        x = x + gt_f * moe_ffn(h, w_router[l], router_bias[l], w_exp_gate[l], w_exp_up[l],
                               w_exp_down[l], w_sh_gate[l], w_sh_up[l], w_sh_down[l])
    return rms_norm(x, g_final)
```

```python
import functools

import numpy as np
import jax
import jax.numpy as jnp
from jax import lax
from jax.experimental import pallas as pl
from jax.experimental.pallas import tpu as pltpu

F32 = jnp.float32
BF16 = jnp.bfloat16
I32 = jnp.int32

HEAD_DIM = 128
N_Q_HEADS = 8
N_KV_GROUPS = 2
HEADS_PER_GROUP = N_Q_HEADS // N_KV_GROUPS
N_BRANCH = 3
CMP_BLOCK = 32
CMP_STRIDE = 16
SEL_BLOCK = 64
SEL_TOPK = 16
WINDOW = 512
ROPE_THETA = 10000.0
CONV_KERNEL = 31
N_EXPERTS = 256
TOP_K = 8
N_EXPERT_GROUPS = 8
TOPK_GROUPS = 4
ROUTE_SCALE = 2.5
DISPATCH_BLOCK = 128
EPS = 1e-6
NEG_INF = -1e30
FORCE = 1e30

LANES = 128
VMEM_LIMIT = 56 * 1024 * 1024

C_GLU = 0
C_Q = 2048
C_KSL = 3072
C_KWN = 3328
C_KC = 3584
C_VC = 3840
C_VSL = 4096
C_VWN = 4352
C_GATE = 4608
IN_PAD = 4864


def _dot(a, b):
    return jnp.dot(a, b, preferred_element_type=F32)


def _dot_nt(a, b):
    return lax.dot_general(a, b, (((1,), (1,)), ((), ())), preferred_element_type=F32)


def _cparams(sem, vmem=VMEM_LIMIT):
    return pltpu.CompilerParams(dimension_semantics=sem, vmem_limit_bytes=vmem)


def _ada_kernel(c_ref, w_ref, b_ref, o_ref):
    c = c_ref[...]
    a = (c * jax.nn.sigmoid(c)).astype(BF16)
    o_ref[...] = _dot(a, w_ref[...].astype(BF16)) + b_ref[...]


def ada_mod(c, w, b):
    B, D = c.shape
    N = w.shape[1]
    tn = 1024
    cp = jnp.pad(c, ((0, 8 - B), (0, 0)))
    out = pl.pallas_call(
        _ada_kernel,
        grid=(N // tn,),
        in_specs=[pl.BlockSpec((8, D), lambda j: (0, 0)),
                  pl.BlockSpec((D, tn), lambda j: (0, j)),
                  pl.BlockSpec((1, tn), lambda j: (0, j))],
        out_specs=pl.BlockSpec((8, tn), lambda j: (0, j)),
        out_shape=jax.ShapeDtypeStruct((8, N), F32),
        compiler_params=_cparams(("arbitrary",)),
        name="ada_mod",
    )(cp, w, b.reshape(1, N))
    return out[:B].reshape(B, 6, D)


def _rope_kernel(pos_ref, inv_ref, sign_ref, cs_ref, sn_ref):
    ang = pos_ref[...].astype(F32) * inv_ref[...]
    cs_ref[...] = jnp.cos(ang)
    sn_ref[...] = jnp.sin(ang) * sign_ref[...]


def rope_table(positions):
    T = positions.size
    half = HEAD_DIM // 2
    inv = ROPE_THETA ** (-jnp.arange(half, dtype=F32) / half)
    inv2 = jnp.concatenate([inv, inv]).reshape(1, HEAD_DIM)
    sign = jnp.concatenate([-jnp.ones((half,), F32), jnp.ones((half,), F32)]).reshape(1, HEAD_DIM)
    tm = min(T, 1024)
    return pl.pallas_call(
        _rope_kernel,
        grid=(T // tm,),
        in_specs=[pl.BlockSpec((tm, 1), lambda i: (i, 0)),
                  pl.BlockSpec((1, HEAD_DIM), lambda i: (0, 0)),
                  pl.BlockSpec((1, HEAD_DIM), lambda i: (0, 0))],
        out_specs=[pl.BlockSpec((tm, HEAD_DIM), lambda i: (i, 0)),
                   pl.BlockSpec((tm, HEAD_DIM), lambda i: (i, 0))],
        out_shape=[jax.ShapeDtypeStruct((T, HEAD_DIM), F32)] * 2,
        compiler_params=_cparams(("arbitrary",)),
        name="rope_table",
    )(positions.reshape(T, 1), inv2, sign)


def _rope_apply(t, cs, sn):
    return t * cs + pltpu.roll(t, HEAD_DIM // 2, 1) * sn


def _prep_w_in(w_in):
    D = w_in.shape[0]
    q = w_in[:, 0:1024]
    kc = w_in[:, 1024:1280]
    vc = w_in[:, 1280:1536]
    ksl = w_in[:, 1536:1792]
    vsl = w_in[:, 1792:2048]
    kwn = w_in[:, 2048:2304]
    vwn = w_in[:, 2304:2560]
    gates = w_in[:, 2560:2584].reshape(D, N_BRANCH, N_KV_GROUPS, HEADS_PER_GROUP)
    glu = w_in[:, 2584:4632]
    gparts = []
    for g in range(N_KV_GROUPS):
        gg = gates[:, :, g, :].reshape(D, N_BRANCH * HEADS_PER_GROUP)
        gparts.append(jnp.pad(gg, ((0, 0), (0, LANES - N_BRANCH * HEADS_PER_GROUP))))
    return jnp.concatenate([glu, q, ksl, kwn, kc, vc, vsl, vwn] + gparts, axis=1).astype(BF16)


def _inproj_kernel(x_ref, g_ref, mod_ref, w_ref, cs_ref, sn_ref, o_ref, hn_ref, *, tn, rope_lo, rope_hi):
    j = pl.program_id(1)

    @pl.when(j == 0)
    def _():
        rows = 256
        for r in range(x_ref.shape[0] // rows):
            x = x_ref[r * rows:(r + 1) * rows, :]
            y = x * lax.rsqrt(jnp.mean(x * x, axis=-1, keepdims=True) + EPS) * g_ref[...]
            h = y * (1.0 + mod_ref[0, 1:2, :]) + mod_ref[0, 0:1, :]
            hn_ref[r * rows:(r + 1) * rows, :] = h.astype(BF16)

    acc = _dot(hn_ref[...], w_ref[...])
    is_rope = jnp.logical_and(j >= rope_lo, j < rope_hi)

    @pl.when(is_rope)
    def _():
        cs = cs_ref[...]
        sn = sn_ref[...]
        parts = [_rope_apply(acc[:, k * LANES:(k + 1) * LANES], cs, sn) for k in range(tn // LANES)]
        o_ref[...] = jnp.concatenate(parts, axis=1).astype(BF16)

    @pl.when(jnp.logical_not(is_rope))
    def _():
        o_ref[...] = acc.astype(BF16)


def in_projection(x2, g_mix, mod, w_in_p, cs, sn, S):
    T, D = x2.shape
    N = w_in_p.shape[1]
    tm = min(S, 1024)
    tn = 256
    per_b = S // tm
    kern = functools.partial(_inproj_kernel, tn=tn, rope_lo=C_Q // tn, rope_hi=C_KC // tn)
    return pl.pallas_call(
        kern,
        grid=(T // tm, N // tn),
        in_specs=[pl.BlockSpec((tm, D), lambda i, j: (i, 0)),
                  pl.BlockSpec((1, D), lambda i, j: (0, 0)),
                  pl.BlockSpec((1, 6, D), lambda i, j: (i // per_b, 0, 0)),
                  pl.BlockSpec((D, tn), lambda i, j: (0, j)),
                  pl.BlockSpec((tm, HEAD_DIM), lambda i, j: (i, 0)),
                  pl.BlockSpec((tm, HEAD_DIM), lambda i, j: (i, 0))],
        out_specs=pl.BlockSpec((tm, tn), lambda i, j: (i, j)),
        out_shape=jax.ShapeDtypeStruct((T, N), BF16),
        scratch_shapes=[pltpu.VMEM((tm, D), BF16)],
        compiler_params=_cparams(("arbitrary", "arbitrary")),
        name="in_projection",
    )(x2, g_mix.reshape(1, D), mod, w_in_p, cs, sn)


def _compress_kernel(z_ref, cs_ref, sn_ref, kpe_ref, kw1_ref, kw2_ref, vpe_ref, vw1_ref, vw2_ref,
                     kc_ref, vc_ref):
    nch = z_ref.shape[1]
    half = CMP_BLOCK // 2 * HEAD_DIM
    cs = pltpu.roll(cs_ref[pl.ds(CMP_STRIDE - 1, nch, stride=CMP_STRIDE), :], nch - 1, 0)
    sn = pltpu.roll(sn_ref[pl.ds(CMP_STRIDE - 1, nch, stride=CMP_STRIDE), :], nch - 1, 0)
    for kv, (pe_ref, w1_ref, w2_ref, out_ref) in enumerate(
            ((kpe_ref, kw1_ref, kw2_ref, kc_ref), (vpe_ref, vw1_ref, vw2_ref, vc_ref))):
        for g in range(N_KV_GROUPS):
            col = kv * 256 + g * HEAD_DIM
            zc = jnp.concatenate(
                [z_ref[0, :, l * 512 + col:l * 512 + col + HEAD_DIM] for l in range(CMP_STRIDE)],
                axis=1).astype(F32)
            a = _dot((zc + pe_ref[:, 0:half]).astype(BF16), w1_ref[0:half, :])
            b = _dot((zc + pe_ref[:, half:2 * half]).astype(BF16), w1_ref[half:2 * half, :])
            flat = a + pltpu.roll(b, nch - 1, 0)
            hid = flat * jax.nn.sigmoid(flat)
            out = _dot(hid.astype(BF16), w2_ref[...])
            if kv == 0:
                out = _rope_apply(out, cs, sn)
            out_ref[0, g] = out.astype(BF16)


def compress_kv(proj, cs, sn, B, S, kpe, kw1, kw2, vpe, vw1, vw2):
    nch = S // CMP_STRIDE
    z = proj[:, C_KC:C_KC + 512].reshape(B, nch, CMP_STRIDE * 512)
    flat_pe = lambda pe: pe.reshape(1, CMP_BLOCK * HEAD_DIM)
    full = lambda a: pl.BlockSpec(a.shape, lambda b: (0,) * a.ndim)
    args = (flat_pe(kpe), kw1.astype(BF16), kw2.astype(BF16), flat_pe(vpe), vw1.astype(BF16), vw2.astype(BF16))
    out_sd = jax.ShapeDtypeStruct((B, N_KV_GROUPS, nch, HEAD_DIM), BF16)
    return pl.pallas_call(
        _compress_kernel,
        grid=(B,),
        in_specs=[pl.BlockSpec((1, nch, CMP_STRIDE * 512), lambda b: (b, 0, 0)),
                  pl.BlockSpec((S, HEAD_DIM), lambda b: (b, 0)),
                  pl.BlockSpec((S, HEAD_DIM), lambda b: (b, 0))] + [full(a) for a in args],
        out_specs=[pl.BlockSpec((1, N_KV_GROUPS, nch, HEAD_DIM), lambda b: (b, 0, 0, 0))] * 2,
        out_shape=[out_sd, out_sd],
        compiler_params=_cparams(("arbitrary",)),
        name="compress_kv",
    )(z, cs, sn, *args)


def _softmax_rows(s, valid):
    m = jnp.max(s, axis=1, keepdims=True)
    p = jnp.where(valid, jnp.exp(s - m), 0.0)
    l = jnp.sum(p, axis=1, keepdims=True)
    return p, l


def _nsa_kernel(q_ref, ksl_ref, vsl_ref, kwn_ref, vwn_ref, kc_ref, vc_ref, gt_ref, ovt_ref, e_ref,
                o_ref, m_scr, l_scr, acc_scr, *, tq, tk, n_sel):
    i = pl.program_id(2)
    t0 = i * tq
    HPG = HEADS_PER_GROUP
    R = HPG * tq
    scale = HEAD_DIM ** -0.5
    q = q_ref[...]
    qs = jnp.concatenate([q[:, h * HEAD_DIM:(h + 1) * HEAD_DIM] for h in range(HPG)], axis=0)

    kc = kc_ref[0, 0]
    vc = vc_ref[0, 0]
    nc = kc.shape[0]
    s_c = _dot_nt(qs, kc) * scale
    row_c = t0 + (lax.broadcasted_iota(I32, (R, nc), 0) & (tq - 1))
    blk_end = lax.broadcasted_iota(I32, (R, nc), 1) * CMP_STRIDE + (CMP_BLOCK - 1)
    valid_c = blk_end <= row_c
    p_c, l_c = _softmax_rows(jnp.where(valid_c, s_c, NEG_INF), valid_c)
    p_c = p_c / jnp.where(l_c > 0.0, l_c, 1.0)
    o_c = _dot(p_c.astype(BF16), vc)

    psum = p_c[0:tq]
    for h in range(1, HPG):
        psum = psum + p_c[h * tq:(h + 1) * tq]
    p_hi = psum.astype(BF16)
    p_lo = (psum - p_hi.astype(F32)).astype(BF16)
    ovt = ovt_ref[...]
    imp = _dot_nt(ovt, p_hi) + _dot_nt(ovt, p_lo)
    jb = lax.broadcasted_iota(I32, (n_sel, tq), 0)
    tt = t0 + lax.broadcasted_iota(I32, (n_sel, tq), 1)
    cur = tt // SEL_BLOCK
    forced = (jb == 0) | (jb == cur) | (jb == cur - 1)
    causal_blk = jb * SEL_BLOCK <= tt
    imp = jnp.where(forced, FORCE, jnp.where(causal_blk, imp, NEG_INF))
    rank = jnp.zeros((n_sel, tq), F32)
    for b in range(n_sel):
        rb = imp[b:b + 1, :]
        beats = (rb > imp) | ((rb == imp) & (jb > b))
        rank = rank + jnp.where(beats, 1.0, 0.0)
    sel_t = jnp.where(rank < float(min(SEL_TOPK, n_sel)), 1.0, 0.0)
    sel_t = jnp.concatenate([sel_t, jnp.zeros((LANES - n_sel, tq), F32)], axis=0)
    sel = sel_t.T.astype(BF16)
    sel4 = jnp.concatenate([sel] * HPG, axis=0)

    m_scr[...] = jnp.full((R, 1), NEG_INF, F32)
    l_scr[...] = jnp.zeros((R, 1), F32)
    acc_scr[...] = jnp.zeros((R, HEAD_DIM), F32)
    diff = lax.broadcasted_iota(I32, (R, tk), 1) - (lax.broadcasted_iota(I32, (R, tk), 0) & (tq - 1))

    def sel_body(kt, carry):
        ks = pl.multiple_of(kt * tk, tk)
        k = ksl_ref[pl.ds(ks, tk), :]
        v = vsl_ref[pl.ds(ks, tk), :]
        s = _dot_nt(qs, k) * scale
        chosen = _dot(sel4, e_ref[kt])
        valid = (chosen > 0.5) & (diff <= t0 - ks)
        s = jnp.where(valid, s, NEG_INF)
        m_old = m_scr[...]
        m_new = jnp.maximum(m_old, jnp.max(s, axis=1, keepdims=True))
        alpha = jnp.exp(m_old - m_new)
        p = jnp.exp(s - m_new)
        l_scr[...] = alpha * l_scr[...] + jnp.sum(p, axis=1, keepdims=True)
        acc_scr[...] = alpha * acc_scr[...] + _dot(p.astype(BF16), v)
        m_scr[...] = m_new
        return carry

    lax.fori_loop(0, (t0 + tq + tk - 1) // tk, sel_body, 0)

    span = WINDOW + tq
    ksw = pl.multiple_of(jnp.maximum(t0 - WINDOW, 0), tq)
    kw = kwn_ref[pl.ds(ksw, span), :]
    vw = vwn_ref[pl.ds(ksw, span), :]
    s_w = _dot_nt(qs, kw) * scale
    rel = (lax.broadcasted_iota(I32, (R, span), 1) - (lax.broadcasted_iota(I32, (R, span), 0) & (tq - 1))
           + (ksw - t0))
    valid_w = (rel <= 0) & (rel > -WINDOW)
    p_w, l_w = _softmax_rows(jnp.where(valid_w, s_w, NEG_INF), valid_w)
    o_w = _dot(p_w.astype(BF16), vw) / l_w

    o_s = acc_scr[...] / l_scr[...]
    gs = jax.nn.sigmoid(gt_ref[...].astype(F32))
    outs = []
    for h in range(HPG):
        sl = slice(h * tq, (h + 1) * tq)
        outs.append(gs[:, h:h + 1] * o_c[sl] + gs[:, HPG + h:HPG + h + 1] * o_s[sl]
                    + gs[:, 2 * HPG + h:2 * HPG + h + 1] * o_w[sl])
    o_ref[...] = jnp.concatenate(outs, axis=1).astype(BF16)


def nsa_attention(proj, k_c, v_c, B, S):
    T = B * S
    tq = 128
    tk = 256
    nq = S // tq
    n_sel = S // SEL_BLOCK
    nc = S // CMP_STRIDE
    assert S >= WINDOW + tq and n_sel <= LANES and S % tk == 0
    cs_ = np.arange(nc) * CMP_STRIDE
    js_ = np.arange(n_sel) * SEL_BLOCK
    overlap = ((cs_[:, None] < js_[None, :] + SEL_BLOCK) & (cs_[:, None] + CMP_BLOCK > js_[None, :]))
    overlap[nc - CMP_BLOCK // CMP_STRIDE + 1:] = False
    ovt = jnp.asarray(overlap.T.astype(np.float32), dtype=BF16)
    keys = np.arange(S)
    e_np = (keys[None, :] // SEL_BLOCK == np.arange(LANES)[:, None]).astype(np.float32)
    e3 = jnp.asarray(e_np.reshape(LANES, S // tk, tk).transpose(1, 0, 2), dtype=BF16)
    qb = C_Q // 512
    cb = lambda c0: c0 // HEAD_DIM
    kern = functools.partial(_nsa_kernel, tq=tq, tk=tk, n_sel=n_sel)
    R = HEADS_PER_GROUP * tq
    kvspec = lambda c0: pl.BlockSpec((S, HEAD_DIM), lambda b, g, i: (b, cb(c0) + g))
    return pl.pallas_call(
        kern,
        grid=(B, N_KV_GROUPS, nq),
        in_specs=[pl.BlockSpec((tq, 512), lambda b, g, i: (b * nq + i, qb + g)),
                  kvspec(C_KSL), kvspec(C_VSL), kvspec(C_KWN), kvspec(C_VWN),
                  pl.BlockSpec((1, 1, nc, HEAD_DIM), lambda b, g, i: (b, g, 0, 0)),
                  pl.BlockSpec((1, 1, nc, HEAD_DIM), lambda b, g, i: (b, g, 0, 0)),
                  pl.BlockSpec((tq, LANES), lambda b, g, i: (b * nq + i, cb(C_GATE) + g)),
                  pl.BlockSpec((n_sel, nc), lambda b, g, i: (0, 0)),
                  pl.BlockSpec((S // tk, LANES, tk), lambda b, g, i: (0, 0, 0))],
        out_specs=pl.BlockSpec((tq, 512), lambda b, g, i: (b * nq + i, g)),
        out_shape=jax.ShapeDtypeStruct((T, N_Q_HEADS * HEAD_DIM), BF16),
        scratch_shapes=[pltpu.VMEM((R, 1), F32), pltpu.VMEM((R, 1), F32), pltpu.VMEM((R, HEAD_DIM), F32)],
        compiler_params=_cparams(("arbitrary", "arbitrary", "arbitrary")),
        name="nsa_attention",
    )(proj, proj, proj, proj, proj, k_c, v_c, proj, ovt, e3)


CONV_HALO = 32


def _conv_kernel(a_ref, b_ref, ha_ref, hb_ref, w_ref, cb_ref, g_ref, be_ref, o_ref, ybuf, cbuf, *, ts, rc):
    i = pl.program_id(1)
    C = a_ref.shape[1]
    a = a_ref[...].astype(F32)
    b = b_ref[...].astype(F32)
    ybuf[CONV_HALO:CONV_HALO + ts, :] = a * jax.nn.sigmoid(b)
    ha = ha_ref[...].astype(F32)
    hb = hb_ref[...].astype(F32)
    ybuf[0:CONV_HALO, :] = jnp.where(i > 0, ha * jax.nn.sigmoid(hb), 0.0)
    off = CONV_HALO - (CONV_KERNEL - 1)
    for r in range(ts // rc):
        for c in range(C // LANES):
            cl = slice(c * LANES, (c + 1) * LANES)
            acc = jnp.broadcast_to(cb_ref[:, cl], (rc, LANES))
            for k in range(CONV_KERNEL):
                acc = acc + w_ref[k:k + 1, cl] * ybuf[r * rc + off + k:r * rc + off + k + rc, cl]
            cbuf[r * rc:(r + 1) * rc, cl] = acc
    y = cbuf[...]
    mu = jnp.mean(y, axis=-1, keepdims=True)
    d = y - mu
    var = jnp.mean(d * d, axis=-1, keepdims=True)
    yn = d * lax.rsqrt(var + EPS) * g_ref[...] + be_ref[...]
    o_ref[...] = (yn * jax.nn.sigmoid(yn)).astype(BF16)


def conformer_conv(proj, conv_w, conv_b, ln_g, ln_b, B, S):
    T = B * S
    C = conv_w.shape[-1]
    ts = 256
    nt = S // ts
    hb_per = ts // CONV_HALO
    kern = functools.partial(_conv_kernel, ts=ts, rc=64)
    halo = lambda cblk: pl.BlockSpec(
        (CONV_HALO, C), lambda b, i: (jnp.maximum((b * nt + i) * hb_per - 1, 0), cblk))
    vec = lambda: pl.BlockSpec((1, C), lambda b, i: (0, 0))
    return pl.pallas_call(
        kern,
        grid=(B, nt),
        in_specs=[pl.BlockSpec((ts, C), lambda b, i: (b * nt + i, 0)),
                  pl.BlockSpec((ts, C), lambda b, i: (b * nt + i, 1)),
                  halo(0), halo(1),
                  pl.BlockSpec((CONV_KERNEL, C), lambda b, i: (0, 0)),
                  vec(), vec(), vec()],
        out_specs=pl.BlockSpec((ts, C), lambda b, i: (b * nt + i, 0)),
        out_shape=jax.ShapeDtypeStruct((T, C), BF16),
        scratch_shapes=[pltpu.VMEM((ts + CONV_HALO, C), F32), pltpu.VMEM((ts, C), F32)],
        compiler_params=_cparams(("arbitrary", "arbitrary")),
        name="conformer_conv",
    )(proj, proj, proj, proj, conv_w.reshape(CONV_KERNEL, C), conv_b.reshape(1, C),
      ln_g.reshape(1, C), ln_b.reshape(1, C))


def _outproj_kernel(oa_ref, oc_ref, w1_ref, w2_ref, x_ref, mod_ref, o_ref):
    acc = _dot(oa_ref[...], w1_ref[...]) + _dot(oc_ref[...], w2_ref[...])
    o_ref[...] = x_ref[...] + mod_ref[0, 2:3, :] * acc


def out_projection(o_attn, o_conv, w_out, x2, mod, S):
    T, D = x2.shape
    K1 = o_attn.shape[1]
    K2 = o_conv.shape[1]
    assert K1 == K2
    tm = min(S, 1024)
    tn = 512
    per_b = S // tm
    wb = w_out.astype(BF16)
    return pl.pallas_call(
        _outproj_kernel,
        grid=(T // tm, D // tn),
        in_specs=[pl.BlockSpec((tm, K1), lambda i, j: (i, 0)),
                  pl.BlockSpec((tm, K2), lambda i, j: (i, 0)),
                  pl.BlockSpec((K1, tn), lambda i, j: (0, j)),
                  pl.BlockSpec((K2, tn), lambda i, j: (1, j)),
                  pl.BlockSpec((tm, tn), lambda i, j: (i, j)),
                  pl.BlockSpec((1, 6, tn), lambda i, j: (i // per_b, 0, j))],
        out_specs=pl.BlockSpec((tm, tn), lambda i, j: (i, j)),
        out_shape=jax.ShapeDtypeStruct((T, D), F32),
        compiler_params=_cparams(("arbitrary", "arbitrary")),
        name="out_projection",
    )(o_attn, o_conv, wb, wb, x2, mod)


def _norm2_kernel(x_ref, g_ref, mod_ref, h_ref, hb_ref):
    x = x_ref[...]
    y = x * lax.rsqrt(jnp.mean(x * x, axis=-1, keepdims=True) + EPS) * g_ref[...]
    h = y * (1.0 + mod_ref[0, 4:5, :]) + mod_ref[0, 3:4, :]
    h_ref[...] = h
    hb_ref[...] = h.astype(BF16)


def ffn_input(x1, g_ffn, mod, S):
    T, D = x1.shape
    tm = min(S, 512)
    per_b = S // tm
    return pl.pallas_call(
        _norm2_kernel,
        grid=(T // tm,),
        in_specs=[pl.BlockSpec((tm, D), lambda i: (i, 0)),
                  pl.BlockSpec((1, D), lambda i: (0, 0)),
                  pl.BlockSpec((1, 6, D), lambda i: (i // per_b, 0, 0))],
        out_specs=[pl.BlockSpec((tm, D), lambda i: (i, 0))] * 2,
        out_shape=[jax.ShapeDtypeStruct((T, D), F32), jax.ShapeDtypeStruct((T, D), BF16)],
        compiler_params=_cparams(("arbitrary",)),
        name="ffn_input",
    )(x1, g_ffn.reshape(1, D), mod)


def _router_kernel(h_ref, wr_ref, bias_ref, tri_ref, idx_ref, w_ref, pos_ref, cnt_ref, carry):
    i = pl.program_id(0)
    E = N_EXPERTS
    tm = h_ref.shape[0]
    gsz = E // N_EXPERT_GROUPS
    BIG = 3.0e38

    @pl.when(i == 0)
    def _():
        carry[...] = jnp.zeros((E, 1), F32)

    h = h_ref[...]
    h_hi = h.astype(BF16)
    h_lo = (h - h_hi.astype(F32)).astype(BF16)
    wr = wr_ref[...]
    w_hi = wr.astype(BF16)
    w_lo = (wr - w_hi.astype(F32)).astype(BF16)
    logits = _dot_nt(w_hi, h_hi) + (_dot_nt(w_hi, h_lo) + _dot_nt(w_lo, h_hi))
    s = jax.nn.sigmoid(logits)
    sb = s + bias_ref[...]
    gi = lax.broadcasted_iota(I32, (gsz, tm), 0)
    gscores = []
    for g in range(N_EXPERT_GROUPS):
        blk = sb[g * gsz:(g + 1) * gsz, :]
        m1 = jnp.max(blk, axis=0, keepdims=True)
        i1 = jnp.min(jnp.where(blk == m1, gi, gsz), axis=0, keepdims=True)
        m2 = jnp.max(jnp.where(gi == i1, -BIG, blk), axis=0, keepdims=True)
        gscores.append(m1 + m2)
    gsc = jnp.concatenate(gscores, axis=0)
    gj = lax.broadcasted_iota(I32, (N_EXPERT_GROUPS, tm), 0)
    grank = jnp.zeros((N_EXPERT_GROUPS, tm), F32)
    for g in range(N_EXPERT_GROUPS):
        rg = gsc[g:g + 1, :]
        beats = (rg > gsc) | ((rg == gsc) & (gj > g))
        grank = grank + jnp.where(beats, 1.0, 0.0)
    gmask = grank < float(TOPK_GROUPS)
    masked = jnp.concatenate(
        [jnp.where(gmask[g:g + 1, :], sb[g * gsz:(g + 1) * gsz, :], NEG_INF) for g in range(N_EXPERT_GROUPS)],
        axis=0)
    ei = lax.broadcasted_iota(I32, (E, tm), 0)
    idxs, vals = [], []
    onehot_sum = jnp.zeros((E, tm), F32)
    for k in range(TOP_K):
        m = jnp.max(masked, axis=0, keepdims=True)
        ik = jnp.min(jnp.where(masked == m, ei, E), axis=0, keepdims=True)
        hit = ei == ik
        vals.append(jnp.sum(jnp.where(hit, s, 0.0), axis=0, keepdims=True))
        idxs.append(ik)
        onehot_sum = onehot_sum + jnp.where(hit, 1.0, 0.0)
        masked = jnp.where(hit, -BIG, masked)
    top_s = jnp.concatenate(vals, axis=0)
    top_i = jnp.concatenate(idxs, axis=0)
    idx_ref[...] = top_i
    w_ref[...] = top_s / jnp.sum(top_s, axis=0, keepdims=True) * ROUTE_SCALE
    before = _dot(onehot_sum.astype(BF16), tri_ref[...]) + carry[...]
    pos_ref[...] = jnp.concatenate(
        [jnp.sum(jnp.where(ei == idxs[k], before, 0.0), axis=0, keepdims=True) for k in range(TOP_K)],
        axis=0).astype(I32)
    carry[...] = carry[...] + jnp.sum(onehot_sum, axis=1, keepdims=True)
    cnt_ref[...] = carry[...]


def router(h, w_router, router_bias):
    T, D = h.shape
    E = N_EXPERTS
    tm = 512 if T % 512 == 0 else T
    tri = jnp.asarray(np.triu(np.ones((tm, tm), np.float32), k=1), dtype=BF16)
    out = pl.pallas_call(
        _router_kernel,
        grid=(T // tm,),
        in_specs=[pl.BlockSpec((tm, D), lambda i: (i, 0)),
                  pl.BlockSpec((E, D), lambda i: (0, 0)),
                  pl.BlockSpec((E, 1), lambda i: (0, 0)),
                  pl.BlockSpec((tm, tm), lambda i: (0, 0))],
        out_specs=[pl.BlockSpec((TOP_K, tm), lambda i: (0, i)),
                   pl.BlockSpec((TOP_K, tm), lambda i: (0, i)),
                   pl.BlockSpec((TOP_K, tm), lambda i: (0, i)),
                   pl.BlockSpec((E, 1), lambda i: (0, 0))],
        out_shape=[jax.ShapeDtypeStruct((TOP_K, T), I32),
                   jax.ShapeDtypeStruct((TOP_K, T), F32),
                   jax.ShapeDtypeStruct((TOP_K, T), I32),
                   jax.ShapeDtypeStruct((E, 1), F32)],
        scratch_shapes=[pltpu.VMEM((E, 1), F32)],
        compiler_params=_cparams(("arbitrary",)),
        name="router",
    )(h, w_router.T, router_bias.reshape(E, 1), tri)
    return out


def _dispatch_kernel(tok_ref, h_hbm, o_ref, sem, *, rb):
    def issue(r, c):
        tok = tok_ref[0, 0, r]
        pltpu.make_async_copy(h_hbm.at[pl.ds(tok, 1), :], o_ref.at[pl.ds(r, 1), :], sem).start()
        return c

    lax.fori_loop(0, rb, issue, 0)
    pltpu.make_async_copy(h_hbm.at[pl.ds(0, rb), :], o_ref, sem).wait()


def dispatch_rows(h, row_tok, rb):
    T, D = h.shape
    P = row_tok.shape[0]
    nb = P // rb
    kern = functools.partial(_dispatch_kernel, rb=rb)
    return pl.pallas_call(
        kern,
        grid=(nb,),
        in_specs=[pl.BlockSpec((1, 1, rb), lambda i: (i, 0, 0), memory_space=pltpu.SMEM),
                  pl.BlockSpec(memory_space=pl.ANY)],
        out_specs=pl.BlockSpec((rb, D), lambda i: (i, 0)),
        out_shape=jax.ShapeDtypeStruct((P, D), h.dtype),
        scratch_shapes=[pltpu.SemaphoreType.DMA(())],
        compiler_params=_cparams(("arbitrary",)),
        name="dispatch_rows",
    )(row_tok.reshape(nb, 1, rb), h)


def _gmm_kernel(be_ref, bv_ref, x_ref, wg_ref, wu_ref, wd_ref, y_ref):
    i = pl.program_id(0)

    @pl.when(bv_ref[i] > 0)
    def _():
        x = x_ref[...].astype(BF16)
        g = _dot(x, wg_ref[0].astype(BF16))
        u = _dot(x, wu_ref[0].astype(BF16))
        a = (g * jax.nn.sigmoid(g) * u).astype(BF16)
        y_ref[...] = _dot(a, wd_ref[0].astype(BF16))

    @pl.when(bv_ref[i] == 0)
    def _():
        y_ref[...] = jnp.zeros(y_ref.shape, F32)


def grouped_experts(xs, block_e, block_valid, w_gate, w_up, w_down):
    P, D = xs.shape
    H = w_gate.shape[-1]
    bm = DISPATCH_BLOCK
    nb = P // bm
    gs = pltpu.PrefetchScalarGridSpec(
        num_scalar_prefetch=2,
        grid=(nb,),
        in_specs=[pl.BlockSpec((bm, D), lambda i, be, bv: (i, 0)),
                  pl.BlockSpec((1, D, H), lambda i, be, bv: (be[i], 0, 0)),
                  pl.BlockSpec((1, D, H), lambda i, be, bv: (be[i], 0, 0)),
                  pl.BlockSpec((1, H, D), lambda i, be, bv: (be[i], 0, 0))],
        out_specs=pl.BlockSpec((bm, D), lambda i, be, bv: (i, 0)),
    )
    return pl.pallas_call(
        _gmm_kernel,
        grid_spec=gs,
        out_shape=jax.ShapeDtypeStruct((P, D), F32),
        compiler_params=_cparams(("arbitrary",)),
        name="grouped_experts",
    )(block_e, block_valid, xs, w_gate, w_up, w_down)


def _shared_kernel(h_ref, wg_ref, wu_ref, wd_ref, o_ref):
    x = h_ref[...]
    g = _dot(x, wg_ref[...])
    u = _dot(x, wu_ref[...])
    a = (g * jax.nn.sigmoid(g) * u).astype(BF16)
    o_ref[...] = _dot(a, wd_ref[...])


def shared_expert(hb, wg, wu, wd):
    T, D = hb.shape
    H = wg.shape[1]
    tm = 512 if T % 512 == 0 else T
    return pl.pallas_call(
        _shared_kernel,
        grid=(T // tm,),
        in_specs=[pl.BlockSpec((tm, D), lambda i: (i, 0)),
                  pl.BlockSpec((D, H), lambda i: (0, 0)),
                  pl.BlockSpec((D, H), lambda i: (0, 0)),
                  pl.BlockSpec((H, D), lambda i: (0, 0))],
        out_specs=pl.BlockSpec((tm, D), lambda i: (i, 0)),
        out_shape=jax.ShapeDtypeStruct((T, D), F32),
        compiler_params=_cparams(("arbitrary",)),
        name="shared_expert",
    )(hb, wg.astype(BF16), wu.astype(BF16), wd.astype(BF16))


def _combine_kernel(dest_ref, y_hbm, w_ref, sh_ref, x_ref, mod_ref, g_ref, o_ref, buf, sem, *, tt):
    for k in range(TOP_K):
        def issue(r, c, k=k):
            d = dest_ref[k, r]
            pltpu.make_async_copy(y_hbm.at[pl.ds(d, 1), :], buf.at[k, pl.ds(r, 1), :], sem).start()
            return c
        lax.fori_loop(0, tt, issue, 0)
    for k in range(TOP_K):
        pltpu.make_async_copy(y_hbm.at[pl.ds(0, tt), :], buf.at[k], sem).wait()
    w = w_ref[...]
    acc = sh_ref[...]
    for k in range(TOP_K):
        acc = acc + w[:, k:k + 1] * buf[k]
    x = x_ref[...] + mod_ref[0, 5:6, :] * acc
    o_ref[...] = x * lax.rsqrt(jnp.mean(x * x, axis=-1, keepdims=True) + EPS) * g_ref[...]


def combine(dest, y, top_w_t, shared, x1, mod, g_final, S):
    T, D = x1.shape
    tt = 128
    per_b = S // tt
    kern = functools.partial(_combine_kernel, tt=tt)
    return pl.pallas_call(
        kern,
        grid=(T // tt,),
        in_specs=[pl.BlockSpec((TOP_K, tt), lambda i: (0, i), memory_space=pltpu.SMEM),
                  pl.BlockSpec(memory_space=pl.ANY),
                  pl.BlockSpec((tt, TOP_K), lambda i: (i, 0)),
                  pl.BlockSpec((tt, D), lambda i: (i, 0)),
                  pl.BlockSpec((tt, D), lambda i: (i, 0)),
                  pl.BlockSpec((1, 6, D), lambda i: (i // per_b, 0, 0)),
                  pl.BlockSpec((1, D), lambda i: (0, 0))],
        out_specs=pl.BlockSpec((tt, D), lambda i: (i, 0)),
        out_shape=jax.ShapeDtypeStruct((T, D), F32),
        scratch_shapes=[pltpu.VMEM((TOP_K, tt, D), F32), pltpu.SemaphoreType.DMA(())],
        compiler_params=_cparams(("arbitrary",)),
        name="combine",
    )(dest, y, top_w_t, shared, x1, mod, g_final.reshape(1, D))


def _dispatch_plan(top_i, pos, counts, T):
    E = N_EXPERTS
    A = T * TOP_K
    cnt = counts.reshape(E).astype(I32)
    padded = (cnt + DISPATCH_BLOCK - 1) // DISPATCH_BLOCK * DISPATCH_BLOCK
    pad_ends = jnp.cumsum(padded)
    pad_starts = pad_ends - padded
    dest = pad_starts[top_i] + pos
    n_blocks = -(-(A + E * (DISPATCH_BLOCK - 1)) // DISPATCH_BLOCK)
    P = n_blocks * DISPATCH_BLOCK
    tok = jnp.broadcast_to(jnp.arange(T, dtype=I32)[None, :], (TOP_K, T))
    row_tok = jnp.zeros((P,), I32).at[dest.reshape(-1)].set(tok.reshape(-1))
    blk_start = jnp.arange(n_blocks, dtype=I32) * DISPATCH_BLOCK
    block_e = jnp.minimum(jnp.searchsorted(pad_ends, blk_start, side='right'), E - 1).astype(I32)
    block_valid = (blk_start < pad_ends[-1]).astype(I32)
    return dest, row_tok, block_e, block_valid


def moe_ffn(x1, mod, g_ffn, w_router, router_bias, w_exp_gate, w_exp_up, w_exp_down,
            w_sh_gate, w_sh_up, w_sh_down, g_final, S):
    T, D = x1.shape
    h, hb = ffn_input(x1, g_ffn, mod, S)
    top_i, top_w, pos, counts = router(h, w_router, router_bias)
    dest, row_tok, block_e, block_valid = _dispatch_plan(top_i, pos, counts, T)
    xs = dispatch_rows(h, row_tok, 2 * DISPATCH_BLOCK)
    y = grouped_experts(xs, block_e, block_valid, w_exp_gate, w_exp_up, w_exp_down)
    shared = shared_expert(hb, w_sh_gate, w_sh_up, w_sh_down)
    return combine(dest, y, top_w.T, shared, x1, mod, g_final, S)


def kernel(x, c, positions, w_ada, b_ada, g_mix, w_in, cmp_k_pe, cmp_k_w1, cmp_k_w2, cmp_v_pe, cmp_v_w1,
           cmp_v_w2, conv_w, conv_b, conv_ln_g, conv_ln_b, w_out, g_ffn, w_router, router_bias, w_exp_gate,
           w_exp_up, w_exp_down, w_sh_gate, w_sh_up, w_sh_down, g_final):
    B, S, D = x.shape
    assert w_ada.shape[0] == 1
    x2 = x.reshape(B * S, D)
    mod = ada_mod(c, w_ada[0], b_ada[0])
    cs, sn = rope_table(positions)
    proj = in_projection(x2, g_mix[0], mod, _prep_w_in(w_in[0]), cs, sn, S)
    k_c, v_c = compress_kv(proj, cs, sn, B, S, cmp_k_pe[0], cmp_k_w1[0], cmp_k_w2[0],
                           cmp_v_pe[0], cmp_v_w1[0], cmp_v_w2[0])
    o_attn = nsa_attention(proj, k_c, v_c, B, S)
    o_conv = conformer_conv(proj, conv_w[0], conv_b[0], conv_ln_g[0], conv_ln_b[0], B, S)
    x1 = out_projection(o_attn, o_conv, w_out[0], x2, mod, S)
    out = moe_ffn(x1, mod, g_ffn[0], w_router[0], router_bias[0], w_exp_gate[0], w_exp_up[0],
                  w_exp_down[0], w_sh_gate[0], w_sh_up[0], w_sh_down[0], g_final, S)
    return out.reshape(B, S, D)
```

```python
import functools

import numpy as np
import jax
import jax.numpy as jnp
from jax import lax
from jax.experimental import pallas as pl
from jax.experimental.pallas import tpu as pltpu

F32 = jnp.float32
BF16 = jnp.bfloat16
I32 = jnp.int32

HEAD_DIM = 128
N_Q_HEADS = 8
N_KV_GROUPS = 2
HEADS_PER_GROUP = N_Q_HEADS // N_KV_GROUPS
N_BRANCH = 3
CMP_BLOCK = 32
CMP_STRIDE = 16
SEL_BLOCK = 64
SEL_TOPK = 16
WINDOW = 512
ROPE_THETA = 10000.0
CONV_KERNEL = 31
N_EXPERTS = 256
TOP_K = 8
N_EXPERT_GROUPS = 8
TOPK_GROUPS = 4
ROUTE_SCALE = 2.5
DISPATCH_BLOCK = 128
EPS = 1e-6
NEG_INF = -1e30
FORCE = 1e30

LANES = 128
VMEM_LIMIT = 56 * 1024 * 1024

C_GLU = 0
C_Q = 2048
C_KSL = 3072
C_KWN = 3328
C_KC = 3584
C_VC = 3840
C_VSL = 4096
C_VWN = 4352
C_GATE = 4608
IN_PAD = 4864


def _dot(a, b):
    return jnp.dot(a, b, preferred_element_type=F32)


def _dot_nt(a, b):
    return lax.dot_general(a, b, (((1,), (1,)), ((), ())), preferred_element_type=F32)


def _cparams(sem, vmem=VMEM_LIMIT):
    return pltpu.CompilerParams(dimension_semantics=sem, vmem_limit_bytes=vmem)


def _ada_kernel(c_ref, w_ref, b_ref, o_ref):
    c = c_ref[...]
    a = (c * jax.nn.sigmoid(c)).astype(BF16)
    o_ref[...] = _dot(a, w_ref[...].astype(BF16)) + b_ref[...]


def ada_mod(c, w, b):
    B, D = c.shape
    N = w.shape[1]
    tn = 1024
    cp = jnp.pad(c, ((0, 8 - B), (0, 0)))
    out = pl.pallas_call(
        _ada_kernel,
        grid=(N // tn,),
        in_specs=[pl.BlockSpec((8, D), lambda j: (0, 0)),
                  pl.BlockSpec((D, tn), lambda j: (0, j)),
                  pl.BlockSpec((1, tn), lambda j: (0, j))],
        out_specs=pl.BlockSpec((8, tn), lambda j: (0, j)),
        out_shape=jax.ShapeDtypeStruct((8, N), F32),
        compiler_params=_cparams(("arbitrary",)),
        name="ada_mod",
    )(cp, w, b.reshape(1, N))
    return out[:B].reshape(B, 6, D)


def _rope_kernel(pos_ref, inv_ref, sign_ref, cs_ref, sn_ref):
    ang = pos_ref[...].astype(F32) * inv_ref[...]
    cs_ref[...] = jnp.cos(ang)
    sn_ref[...] = jnp.sin(ang) * sign_ref[...]


def rope_table(positions):
    T = positions.size
    half = HEAD_DIM // 2
    inv = ROPE_THETA ** (-jnp.arange(half, dtype=F32) / half)
    inv2 = jnp.concatenate([inv, inv]).reshape(1, HEAD_DIM)
    sign = jnp.concatenate([-jnp.ones((half,), F32), jnp.ones((half,), F32)]).reshape(1, HEAD_DIM)
    tm = min(T, 1024)
    return pl.pallas_call(
        _rope_kernel,
        grid=(T // tm,),
        in_specs=[pl.BlockSpec((tm, 1), lambda i: (i, 0)),
                  pl.BlockSpec((1, HEAD_DIM), lambda i: (0, 0)),
                  pl.BlockSpec((1, HEAD_DIM), lambda i: (0, 0))],
        out_specs=[pl.BlockSpec((tm, HEAD_DIM), lambda i: (i, 0)),
                   pl.BlockSpec((tm, HEAD_DIM), lambda i: (i, 0))],
        out_shape=[jax.ShapeDtypeStruct((T, HEAD_DIM), F32)] * 2,
        compiler_params=_cparams(("arbitrary",)),
        name="rope_table",
    )(positions.reshape(T, 1), inv2, sign)


def _rope_apply(t, cs, sn):
    return t * cs + pltpu.roll(t, HEAD_DIM // 2, 1) * sn


def _prep_w_in(w_in):
    D = w_in.shape[0]
    q = w_in[:, 0:1024]
    kc = w_in[:, 1024:1280]
    vc = w_in[:, 1280:1536]
    ksl = w_in[:, 1536:1792]
    vsl = w_in[:, 1792:2048]
    kwn = w_in[:, 2048:2304]
    vwn = w_in[:, 2304:2560]
    gates = w_in[:, 2560:2584].reshape(D, N_BRANCH, N_KV_GROUPS, HEADS_PER_GROUP)
    glu = w_in[:, 2584:4632]
    gparts = []
    for g in range(N_KV_GROUPS):
        gg = gates[:, :, g, :].reshape(D, N_BRANCH * HEADS_PER_GROUP)
        gparts.append(jnp.pad(gg, ((0, 0), (0, LANES - N_BRANCH * HEADS_PER_GROUP))))
    return jnp.concatenate([glu, q, ksl, kwn, kc, vc, vsl, vwn] + gparts, axis=1).astype(BF16)


def _inproj_kernel(x_ref, g_ref, mod_ref, w_ref, cs_ref, sn_ref, o_ref, hn_ref, *, tn, rope_lo, rope_hi, q_hi):
    j = pl.program_id(1)

    @pl.when(j == 0)
    def _():
        rows = 256
        for r in range(x_ref.shape[0] // rows):
            x = x_ref[r * rows:(r + 1) * rows, :]
            y = x * lax.rsqrt(jnp.mean(x * x, axis=-1, keepdims=True) + EPS) * g_ref[...]
            h = y * (1.0 + mod_ref[0, 1:2, :]) + mod_ref[0, 0:1, :]
            hn_ref[r * rows:(r + 1) * rows, :] = h.astype(BF16)

    acc = _dot(hn_ref[...], w_ref[...])
    is_rope = jnp.logical_and(j >= rope_lo, j < rope_hi)

    @pl.when(is_rope)
    def _():
        qs = jnp.where(j < q_hi, HEAD_DIM ** -0.5, 1.0)
        cs = cs_ref[...] * qs
        sn = sn_ref[...] * qs
        parts = [_rope_apply(acc[:, k * LANES:(k + 1) * LANES], cs, sn) for k in range(tn // LANES)]
        o_ref[...] = jnp.concatenate(parts, axis=1).astype(BF16)

    @pl.when(jnp.logical_not(is_rope))
    def _():
        o_ref[...] = acc.astype(BF16)


def in_projection(x2, g_mix, mod, w_in_p, cs, sn, S):
    T, D = x2.shape
    N = w_in_p.shape[1]
    tm = min(S, 1024)
    tn = 256
    per_b = S // tm
    kern = functools.partial(_inproj_kernel, tn=tn, rope_lo=C_Q // tn, rope_hi=C_KC // tn, q_hi=C_KSL // tn)
    return pl.pallas_call(
        kern,
        grid=(T // tm, N // tn),
        in_specs=[pl.BlockSpec((tm, D), lambda i, j: (i, 0)),
                  pl.BlockSpec((1, D), lambda i, j: (0, 0)),
                  pl.BlockSpec((1, 6, D), lambda i, j: (i // per_b, 0, 0)),
                  pl.BlockSpec((D, tn), lambda i, j: (0, j)),
                  pl.BlockSpec((tm, HEAD_DIM), lambda i, j: (i, 0)),
                  pl.BlockSpec((tm, HEAD_DIM), lambda i, j: (i, 0))],
        out_specs=pl.BlockSpec((tm, tn), lambda i, j: (i, j)),
        out_shape=jax.ShapeDtypeStruct((T, N), BF16),
        scratch_shapes=[pltpu.VMEM((tm, D), BF16)],
        compiler_params=_cparams(("arbitrary", "arbitrary")),
        name="in_projection",
    )(x2, g_mix.reshape(1, D), mod, w_in_p, cs, sn)


def _compress_kernel(z_ref, cs_ref, sn_ref, kpe_ref, kw1_ref, kw2_ref, vpe_ref, vw1_ref, vw2_ref,
                     kc_ref, vc_ref):
    nch = z_ref.shape[1]
    half = CMP_BLOCK // 2 * HEAD_DIM
    cs = pltpu.roll(cs_ref[pl.ds(CMP_STRIDE - 1, nch, stride=CMP_STRIDE), :], nch - 1, 0)
    sn = pltpu.roll(sn_ref[pl.ds(CMP_STRIDE - 1, nch, stride=CMP_STRIDE), :], nch - 1, 0)
    for kv, (pe_ref, w1_ref, w2_ref, out_ref) in enumerate(
            ((kpe_ref, kw1_ref, kw2_ref, kc_ref), (vpe_ref, vw1_ref, vw2_ref, vc_ref))):
        for g in range(N_KV_GROUPS):
            col = kv * 256 + g * HEAD_DIM
            zc = jnp.concatenate(
                [z_ref[0, :, l * 512 + col:l * 512 + col + HEAD_DIM] for l in range(CMP_STRIDE)],
                axis=1).astype(F32)
            a = _dot((zc + pe_ref[:, 0:half]).astype(BF16), w1_ref[0:half, :])
            b = _dot((zc + pe_ref[:, half:2 * half]).astype(BF16), w1_ref[half:2 * half, :])
            flat = a + pltpu.roll(b, nch - 1, 0)
            hid = flat * jax.nn.sigmoid(flat)
            out = _dot(hid.astype(BF16), w2_ref[...])
            if kv == 0:
                out = _rope_apply(out, cs, sn)
            out_ref[0, g] = out.astype(BF16)


def compress_kv(proj, cs, sn, B, S, kpe, kw1, kw2, vpe, vw1, vw2):
    nch = S // CMP_STRIDE
    z = proj[:, C_KC:C_KC + 512].reshape(B, nch, CMP_STRIDE * 512)
    flat_pe = lambda pe: pe.reshape(1, CMP_BLOCK * HEAD_DIM)
    full = lambda a: pl.BlockSpec(a.shape, lambda b: (0,) * a.ndim)
    args = (flat_pe(kpe), kw1.astype(BF16), kw2.astype(BF16), flat_pe(vpe), vw1.astype(BF16), vw2.astype(BF16))
    out_sd = jax.ShapeDtypeStruct((B, N_KV_GROUPS, nch, HEAD_DIM), BF16)
    return pl.pallas_call(
        _compress_kernel,
        grid=(B,),
        in_specs=[pl.BlockSpec((1, nch, CMP_STRIDE * 512), lambda b: (b, 0, 0)),
                  pl.BlockSpec((S, HEAD_DIM), lambda b: (b, 0)),
                  pl.BlockSpec((S, HEAD_DIM), lambda b: (b, 0))] + [full(a) for a in args],
        out_specs=[pl.BlockSpec((1, N_KV_GROUPS, nch, HEAD_DIM), lambda b: (b, 0, 0, 0))] * 2,
        out_shape=[out_sd, out_sd],
        compiler_params=_cparams(("arbitrary",)),
        name="compress_kv",
    )(z, cs, sn, *args)


def _softmax_rows(s, valid):
    m = jnp.max(s, axis=1, keepdims=True)
    p = jnp.where(valid, jnp.exp(s - m), 0.0)
    l = jnp.sum(p, axis=1, keepdims=True)
    return p, l


def _nsa_kernel(q_ref, ksl_ref, vsl_ref, kwn_ref, vwn_ref, kc_ref, vc_ref, gt_ref, ovt_ref, e_ref,
                o_ref, m_scr, l_scr, acc_scr, *, tq, tk, n_sel):
    i = pl.program_id(2)
    t0 = i * tq
    HPG = HEADS_PER_GROUP
    R = HPG * tq
    q = q_ref[...]
    qs = jnp.concatenate([q[:, h * HEAD_DIM:(h + 1) * HEAD_DIM] for h in range(HPG)], axis=0)

    kc = kc_ref[0, 0]
    vc = vc_ref[0, 0]
    nc = kc.shape[0]
    s_c = _dot_nt(qs, kc)
    row_c = t0 + (lax.broadcasted_iota(I32, (R, nc), 0) & (tq - 1))
    blk_end = lax.broadcasted_iota(I32, (R, nc), 1) * CMP_STRIDE + (CMP_BLOCK - 1)
    valid_c = blk_end <= row_c
    p_c, l_c = _softmax_rows(jnp.where(valid_c, s_c, NEG_INF), valid_c)
    p_c = p_c / jnp.where(l_c > 0.0, l_c, 1.0)
    o_c = _dot(p_c.astype(BF16), vc)

    psum = p_c[0:tq]
    for h in range(1, HPG):
        psum = psum + p_c[h * tq:(h + 1) * tq]
    p_hi = psum.astype(BF16)
    p_lo = (psum - p_hi.astype(F32)).astype(BF16)
    ovt = ovt_ref[...]
    imp = _dot_nt(ovt, p_hi) + _dot_nt(ovt, p_lo)
    jb = lax.broadcasted_iota(I32, (n_sel, tq), 0)
    tt = t0 + lax.broadcasted_iota(I32, (n_sel, tq), 1)
    cur = tt // SEL_BLOCK
    forced = (jb == 0) | (jb == cur) | (jb == cur - 1)
    causal_blk = jb * SEL_BLOCK <= tt
    imp = jnp.where(forced, FORCE, jnp.where(causal_blk, imp, NEG_INF))
    rank = jnp.zeros((n_sel, tq), F32)
    for b in range(n_sel):
        rb = imp[b:b + 1, :]
        beats = (rb > imp) | ((rb == imp) & (jb > b))
        rank = rank + jnp.where(beats, 1.0, 0.0)
    selb_t = jnp.where(rank < float(min(SEL_TOPK, n_sel)), 0.0, NEG_INF)
    selb_t = jnp.concatenate([selb_t, jnp.zeros((LANES - n_sel, tq), F32)], axis=0)
    selb = selb_t.T.astype(BF16)
    selb4 = jnp.concatenate([selb] * HPG, axis=0)

    m_scr[...] = jnp.full((R, LANES), NEG_INF, F32)
    l_scr[...] = jnp.zeros((R, LANES), F32)
    acc_scr[...] = jnp.zeros((R, HEAD_DIM), F32)

    def sel_tile(kt, on_diagonal):
        ks = pl.multiple_of(kt * tk, tk)
        k = ksl_ref[pl.ds(ks, tk), :]
        v = vsl_ref[pl.ds(ks, tk), :]
        s = _dot_nt(qs, k) + _dot(selb4, e_ref[kt])
        if on_diagonal:
            diff = lax.broadcasted_iota(I32, (R, tk), 1) - (lax.broadcasted_iota(I32, (R, tk), 0) & (tq - 1))
            s = jnp.where(diff <= t0 - ks, s, NEG_INF)
        m_old = m_scr[...]
        m_new = jnp.maximum(m_old, jnp.max(s, axis=1, keepdims=True))
        alpha = jnp.exp(m_old - m_new)
        p = jnp.exp(s - jnp.concatenate([m_new] * (tk // LANES), axis=1))
        l_scr[...] = alpha * l_scr[...] + jnp.sum(p, axis=1, keepdims=True)
        acc_scr[...] = alpha * acc_scr[...] + _dot(p.astype(BF16), v)
        m_scr[...] = m_new

    n_full = t0 // tk

    def sel_body(kt, carry):
        sel_tile(kt, False)
        return carry

    lax.fori_loop(0, n_full, sel_body, 0)
    sel_tile(n_full, True)

    span = WINDOW + tq
    ksw = pl.multiple_of(jnp.maximum(t0 - WINDOW, 0), tq)
    kw = kwn_ref[pl.ds(ksw, span), :]
    vw = vwn_ref[pl.ds(ksw, span), :]
    s_w = _dot_nt(qs, kw)
    rel = (lax.broadcasted_iota(I32, (R, span), 1) - (lax.broadcasted_iota(I32, (R, span), 0) & (tq - 1))
           + (ksw - t0))
    valid_w = (rel <= 0) & (rel > -WINDOW)
    p_w, l_w = _softmax_rows(jnp.where(valid_w, s_w, NEG_INF), valid_w)
    o_w = _dot(p_w.astype(BF16), vw) / l_w

    o_s = acc_scr[...] / l_scr[...]
    gs = jax.nn.sigmoid(gt_ref[...].astype(F32))
    outs = []
    for h in range(HPG):
        sl = slice(h * tq, (h + 1) * tq)
        outs.append(gs[:, h:h + 1] * o_c[sl] + gs[:, HPG + h:HPG + h + 1] * o_s[sl]
                    + gs[:, 2 * HPG + h:2 * HPG + h + 1] * o_w[sl])
    o_ref[...] = jnp.concatenate(outs, axis=1).astype(BF16)


def nsa_attention(proj, k_c, v_c, B, S):
    T = B * S
    tq = 128
    tk = 256
    nq = S // tq
    n_sel = S // SEL_BLOCK
    nc = S // CMP_STRIDE
    assert S >= WINDOW + tq and n_sel <= LANES and S % tk == 0
    cs_ = np.arange(nc) * CMP_STRIDE
    js_ = np.arange(n_sel) * SEL_BLOCK
    overlap = ((cs_[:, None] < js_[None, :] + SEL_BLOCK) & (cs_[:, None] + CMP_BLOCK > js_[None, :]))
    overlap[nc - CMP_BLOCK // CMP_STRIDE + 1:] = False
    ovt = jnp.asarray(overlap.T.astype(np.float32), dtype=BF16)
    keys = np.arange(S)
    e_np = (keys[None, :] // SEL_BLOCK == np.arange(LANES)[:, None]).astype(np.float32)
    e3 = jnp.asarray(e_np.reshape(LANES, S // tk, tk).transpose(1, 0, 2), dtype=BF16)
    qb = C_Q // 512
    cb = lambda c0: c0 // HEAD_DIM
    kern = functools.partial(_nsa_kernel, tq=tq, tk=tk, n_sel=n_sel)
    R = HEADS_PER_GROUP * tq
    kvspec = lambda c0: pl.BlockSpec((S, HEAD_DIM), lambda b, g, i: (b, cb(c0) + g))
    return pl.pallas_call(
        kern,
        grid=(B, N_KV_GROUPS, nq),
        in_specs=[pl.BlockSpec((tq, 512), lambda b, g, i: (b * nq + i, qb + g)),
                  kvspec(C_KSL), kvspec(C_VSL), kvspec(C_KWN), kvspec(C_VWN),
                  pl.BlockSpec((1, 1, nc, HEAD_DIM), lambda b, g, i: (b, g, 0, 0)),
                  pl.BlockSpec((1, 1, nc, HEAD_DIM), lambda b, g, i: (b, g, 0, 0)),
                  pl.BlockSpec((tq, LANES), lambda b, g, i: (b * nq + i, cb(C_GATE) + g)),
                  pl.BlockSpec((n_sel, nc), lambda b, g, i: (0, 0)),
                  pl.BlockSpec((S // tk, LANES, tk), lambda b, g, i: (0, 0, 0))],
        out_specs=pl.BlockSpec((tq, 512), lambda b, g, i: (b * nq + i, g)),
        out_shape=jax.ShapeDtypeStruct((T, N_Q_HEADS * HEAD_DIM), BF16),
        scratch_shapes=[pltpu.VMEM((R, LANES), F32), pltpu.VMEM((R, LANES), F32),
                        pltpu.VMEM((R, HEAD_DIM), F32)],
        compiler_params=_cparams(("arbitrary", "arbitrary", "arbitrary")),
        name="nsa_attention",
    )(proj, proj, proj, proj, proj, k_c, v_c, proj, ovt, e3)


CONV_HALO = 32


def _conv_kernel(a_ref, b_ref, ha_ref, hb_ref, w_ref, cb_ref, g_ref, be_ref, o_ref, ybuf, cbuf, *, ts, rc):
    i = pl.program_id(1)
    C = a_ref.shape[1]
    a = a_ref[...].astype(F32)
    b = b_ref[...].astype(F32)
    ybuf[CONV_HALO:CONV_HALO + ts, :] = a * jax.nn.sigmoid(b)
    ha = ha_ref[...].astype(F32)
    hb = hb_ref[...].astype(F32)
    ybuf[0:CONV_HALO, :] = jnp.where(i > 0, ha * jax.nn.sigmoid(hb), 0.0)
    off = CONV_HALO - (CONV_KERNEL - 1)
    for r in range(ts // rc):
        for c in range(C // LANES):
            cl = slice(c * LANES, (c + 1) * LANES)
            acc = jnp.broadcast_to(cb_ref[:, cl], (rc, LANES))
            for k in range(CONV_KERNEL):
                acc = acc + w_ref[k:k + 1, cl] * ybuf[r * rc + off + k:r * rc + off + k + rc, cl]
            cbuf[r * rc:(r + 1) * rc, cl] = acc
    y = cbuf[...]
    mu = jnp.mean(y, axis=-1, keepdims=True)
    d = y - mu
    var = jnp.mean(d * d, axis=-1, keepdims=True)
    yn = d * lax.rsqrt(var + EPS) * g_ref[...] + be_ref[...]
    o_ref[...] = (yn * jax.nn.sigmoid(yn)).astype(BF16)


def conformer_conv(proj, conv_w, conv_b, ln_g, ln_b, B, S):
    T = B * S
    C = conv_w.shape[-1]
    ts = 256
    nt = S // ts
    hb_per = ts // CONV_HALO
    kern = functools.partial(_conv_kernel, ts=ts, rc=64)
    halo = lambda cblk: pl.BlockSpec(
        (CONV_HALO, C), lambda b, i: (jnp.maximum((b * nt + i) * hb_per - 1, 0), cblk))
    vec = lambda: pl.BlockSpec((1, C), lambda b, i: (0, 0))
    return pl.pallas_call(
        kern,
        grid=(B, nt),
        in_specs=[pl.BlockSpec((ts, C), lambda b, i: (b * nt + i, 0)),
                  pl.BlockSpec((ts, C), lambda b, i: (b * nt + i, 1)),
                  halo(0), halo(1),
                  pl.BlockSpec((CONV_KERNEL, C), lambda b, i: (0, 0)),
                  vec(), vec(), vec()],
        out_specs=pl.BlockSpec((ts, C), lambda b, i: (b * nt + i, 0)),
        out_shape=jax.ShapeDtypeStruct((T, C), BF16),
        scratch_shapes=[pltpu.VMEM((ts + CONV_HALO, C), F32), pltpu.VMEM((ts, C), F32)],
        compiler_params=_cparams(("arbitrary", "arbitrary")),
        name="conformer_conv",
    )(proj, proj, proj, proj, conv_w.reshape(CONV_KERNEL, C), conv_b.reshape(1, C),
      ln_g.reshape(1, C), ln_b.reshape(1, C))


def _outproj_kernel(oa_ref, oc_ref, w1_ref, w2_ref, x_ref, mod_ref, o_ref):
    acc = _dot(oa_ref[...], w1_ref[...]) + _dot(oc_ref[...], w2_ref[...])
    o_ref[...] = x_ref[...] + mod_ref[0, 2:3, :] * acc


def out_projection(o_attn, o_conv, w_out, x2, mod, S):
    T, D = x2.shape
    K1 = o_attn.shape[1]
    K2 = o_conv.shape[1]
    assert K1 == K2
    tm = min(S, 1024)
    tn = 512
    per_b = S // tm
    wb = w_out.astype(BF16)
    return pl.pallas_call(
        _outproj_kernel,
        grid=(T // tm, D // tn),
        in_specs=[pl.BlockSpec((tm, K1), lambda i, j: (i, 0)),
                  pl.BlockSpec((tm, K2), lambda i, j: (i, 0)),
                  pl.BlockSpec((K1, tn), lambda i, j: (0, j)),
                  pl.BlockSpec((K2, tn), lambda i, j: (1, j)),
                  pl.BlockSpec((tm, tn), lambda i, j: (i, j)),
                  pl.BlockSpec((1, 6, tn), lambda i, j: (i // per_b, 0, j))],
        out_specs=pl.BlockSpec((tm, tn), lambda i, j: (i, j)),
        out_shape=jax.ShapeDtypeStruct((T, D), F32),
        compiler_params=_cparams(("arbitrary", "arbitrary")),
        name="out_projection",
    )(o_attn, o_conv, wb, wb, x2, mod)


def _norm2_kernel(x_ref, g_ref, mod_ref, h_ref, hb_ref, h3_ref):
    x = x_ref[...]
    y = x * lax.rsqrt(jnp.mean(x * x, axis=-1, keepdims=True) + EPS) * g_ref[...]
    h = y * (1.0 + mod_ref[0, 4:5, :]) + mod_ref[0, 3:4, :]
    h_ref[...] = h
    hb_ref[...] = h.astype(BF16)
    SL = h.shape[1] // LANES
    for s in range(SL):
        h3_ref[pl.ds(s, h.shape[0], stride=SL), :] = h[:, s * LANES:(s + 1) * LANES]


def ffn_input(x1, g_ffn, mod, S):
    T, D = x1.shape
    tm = min(S, 512)
    per_b = S // tm
    return pl.pallas_call(
        _norm2_kernel,
        grid=(T // tm,),
        in_specs=[pl.BlockSpec((tm, D), lambda i: (i, 0)),
                  pl.BlockSpec((1, D), lambda i: (0, 0)),
                  pl.BlockSpec((1, 6, D), lambda i: (i // per_b, 0, 0))],
        out_specs=[pl.BlockSpec((tm, D), lambda i: (i, 0)),
                   pl.BlockSpec((tm, D), lambda i: (i, 0)),
                   pl.BlockSpec((tm * (D // LANES), LANES), lambda i: (i, 0))],
        out_shape=[jax.ShapeDtypeStruct((T, D), F32), jax.ShapeDtypeStruct((T, D), BF16),
                   jax.ShapeDtypeStruct((T * (D // LANES), LANES), F32)],
        compiler_params=_cparams(("arbitrary",)),
        name="ffn_input",
    )(x1, g_ffn.reshape(1, D), mod)


def _router_kernel(h_ref, wr_ref, bias_ref, tri_ref, idx_ref, w_ref, pos_ref, cnt_ref, carry):
    i = pl.program_id(0)
    E = N_EXPERTS
    tm = h_ref.shape[0]
    gsz = E // N_EXPERT_GROUPS
    BIG = 3.0e38

    @pl.when(i == 0)
    def _():
        carry[...] = jnp.zeros((E, 1), F32)

    h = h_ref[...]
    h_hi = h.astype(BF16)
    h_lo = (h - h_hi.astype(F32)).astype(BF16)
    wr = wr_ref[...]
    w_hi = wr.astype(BF16)
    w_lo = (wr - w_hi.astype(F32)).astype(BF16)
    logits = _dot_nt(w_hi, h_hi) + (_dot_nt(w_hi, h_lo) + _dot_nt(w_lo, h_hi))
    s = jax.nn.sigmoid(logits)
    sb = s + bias_ref[...]
    gi = lax.broadcasted_iota(I32, (gsz, tm), 0)
    gscores = []
    for g in range(N_EXPERT_GROUPS):
        blk = sb[g * gsz:(g + 1) * gsz, :]
        m1 = jnp.max(blk, axis=0, keepdims=True)
        i1 = jnp.min(jnp.where(blk == m1, gi, gsz), axis=0, keepdims=True)
        m2 = jnp.max(jnp.where(gi == i1, -BIG, blk), axis=0, keepdims=True)
        gscores.append(m1 + m2)
    gsc = jnp.concatenate(gscores, axis=0)
    gj = lax.broadcasted_iota(I32, (N_EXPERT_GROUPS, tm), 0)
    grank = jnp.zeros((N_EXPERT_GROUPS, tm), F32)
    for g in range(N_EXPERT_GROUPS):
        rg = gsc[g:g + 1, :]
        beats = (rg > gsc) | ((rg == gsc) & (gj > g))
        grank = grank + jnp.where(beats, 1.0, 0.0)
    gmask = grank < float(TOPK_GROUPS)
    masked = jnp.concatenate(
        [jnp.where(gmask[g:g + 1, :], sb[g * gsz:(g + 1) * gsz, :], NEG_INF) for g in range(N_EXPERT_GROUPS)],
        axis=0)
    ei = lax.broadcasted_iota(I32, (E, tm), 0)
    idxs, vals = [], []
    onehot_sum = jnp.zeros((E, tm), F32)
    for k in range(TOP_K):
        m = jnp.max(masked, axis=0, keepdims=True)
        ik = jnp.min(jnp.where(masked == m, ei, E), axis=0, keepdims=True)
        hit = ei == ik
        vals.append(jnp.sum(jnp.where(hit, s, 0.0), axis=0, keepdims=True))
        idxs.append(ik)
        onehot_sum = onehot_sum + jnp.where(hit, 1.0, 0.0)
        masked = jnp.where(hit, -BIG, masked)
    top_s = jnp.concatenate(vals, axis=0)
    top_i = jnp.concatenate(idxs, axis=0)
    idx_ref[...] = top_i
    w_ref[...] = top_s / jnp.sum(top_s, axis=0, keepdims=True) * ROUTE_SCALE
    before = _dot(onehot_sum.astype(BF16), tri_ref[...]) + carry[...]
    pos_ref[...] = jnp.concatenate(
        [jnp.sum(jnp.where(ei == idxs[k], before, 0.0), axis=0, keepdims=True) for k in range(TOP_K)],
        axis=0).astype(I32)
    carry[...] = carry[...] + jnp.sum(onehot_sum, axis=1, keepdims=True)
    cnt_ref[...] = carry[...]


def router(h, w_router, router_bias):
    T, D = h.shape
    E = N_EXPERTS
    tm = 512 if T % 512 == 0 else T
    tri = jnp.asarray(np.triu(np.ones((tm, tm), np.float32), k=1), dtype=BF16)
    out = pl.pallas_call(
        _router_kernel,
        grid=(T // tm,),
        in_specs=[pl.BlockSpec((tm, D), lambda i: (i, 0)),
                  pl.BlockSpec((E, D), lambda i: (0, 0)),
                  pl.BlockSpec((E, 1), lambda i: (0, 0)),
                  pl.BlockSpec((tm, tm), lambda i: (0, 0))],
        out_specs=[pl.BlockSpec((TOP_K, tm), lambda i: (0, i)),
                   pl.BlockSpec((TOP_K, tm), lambda i: (0, i)),
                   pl.BlockSpec((TOP_K, tm), lambda i: (0, i)),
                   pl.BlockSpec((E, 1), lambda i: (0, 0))],
        out_shape=[jax.ShapeDtypeStruct((TOP_K, T), I32),
                   jax.ShapeDtypeStruct((TOP_K, T), F32),
                   jax.ShapeDtypeStruct((TOP_K, T), I32),
                   jax.ShapeDtypeStruct((E, 1), F32)],
        scratch_shapes=[pltpu.VMEM((E, 1), F32)],
        compiler_params=_cparams(("arbitrary",)),
        name="router",
    )(h, w_router.T, router_bias.reshape(E, 1), tri)
    return out


def _dispatch_kernel(tok_ref, h_hbm, o_ref, sem, *, rb):
    def issue(r, c):
        tok = tok_ref[0, 0, r]
        pltpu.make_async_copy(h_hbm.at[tok], o_ref.at[r], sem).start()
        return c

    lax.fori_loop(0, rb, issue, 0, unroll=8)
    pltpu.make_async_copy(h_hbm.at[pl.ds(0, rb)], o_ref, sem).wait()


def dispatch_rows(h3, row_tok, rb):
    T, SL, L = h3.shape
    P = row_tok.shape[0]
    nb = P // rb
    kern = functools.partial(_dispatch_kernel, rb=rb)
    return pl.pallas_call(
        kern,
        grid=(nb,),
        in_specs=[pl.BlockSpec((1, 1, rb), lambda i: (i, 0, 0), memory_space=pltpu.SMEM),
                  pl.BlockSpec(memory_space=pl.ANY)],
        out_specs=pl.BlockSpec((rb, SL, L), lambda i: (i, 0, 0)),
        out_shape=jax.ShapeDtypeStruct((P, SL, L), h3.dtype),
        scratch_shapes=[pltpu.SemaphoreType.DMA(())],
        compiler_params=_cparams(("arbitrary",)),
        name="dispatch_rows",
    )(row_tok.reshape(nb, 1, rb), h3)


def _gmm_kernel(be_ref, bv_ref, x_ref, wg_ref, wu_ref, wd_ref, y_ref, wgb, wub, wdb):
    i = pl.program_id(0)
    D = wgb.shape[0]
    SL = D // LANES
    bm = x_ref.shape[0] // SL
    fresh = jnp.logical_or(i == 0, be_ref[i] != be_ref[jnp.maximum(i - 1, 0)])

    @pl.when(fresh)
    def _():
        wgb[...] = wg_ref[0].astype(BF16)
        wub[...] = wu_ref[0].astype(BF16)
        wdb[...] = wd_ref[0].astype(BF16)

    @pl.when(bv_ref[i] > 0)
    def _():
        x = jnp.concatenate([x_ref[pl.ds(s, bm, stride=SL), :] for s in range(SL)], axis=1).astype(BF16)
        g = _dot(x, wgb[...])
        u = _dot(x, wub[...])
        a = (g * jax.nn.sigmoid(g) * u).astype(BF16)
        y = _dot(a, wdb[...])
        for s in range(SL):
            y_ref[pl.ds(s, bm, stride=SL), :] = y[:, s * LANES:(s + 1) * LANES]

    @pl.when(bv_ref[i] == 0)
    def _():
        y_ref[...] = jnp.zeros(y_ref.shape, F32)


def grouped_experts(xs, block_e, block_valid, w_gate, w_up, w_down):
    P, SL, L = xs.shape
    D = SL * L
    H = w_gate.shape[-1]
    bm = DISPATCH_BLOCK
    nb = P // bm
    gs = pltpu.PrefetchScalarGridSpec(
        num_scalar_prefetch=2,
        grid=(nb,),
        in_specs=[pl.BlockSpec((bm * SL, L), lambda i, be, bv: (i, 0)),
                  pl.BlockSpec((1, D, H), lambda i, be, bv: (be[i], 0, 0)),
                  pl.BlockSpec((1, D, H), lambda i, be, bv: (be[i], 0, 0)),
                  pl.BlockSpec((1, H, D), lambda i, be, bv: (be[i], 0, 0))],
        out_specs=pl.BlockSpec((bm * SL, L), lambda i, be, bv: (i, 0)),
        scratch_shapes=[pltpu.VMEM((D, H), BF16), pltpu.VMEM((D, H), BF16), pltpu.VMEM((H, D), BF16)],
    )
    return pl.pallas_call(
        _gmm_kernel,
        grid_spec=gs,
        out_shape=jax.ShapeDtypeStruct((P * SL, L), F32),
        compiler_params=_cparams(("arbitrary",)),
        name="grouped_experts",
    )(block_e, block_valid, xs.reshape(P * SL, L), w_gate, w_up, w_down).reshape(P, SL, L)


def _shared_kernel(h_ref, wg_ref, wu_ref, wd_ref, o_ref):
    x = h_ref[...]
    g = _dot(x, wg_ref[...])
    u = _dot(x, wu_ref[...])
    a = (g * jax.nn.sigmoid(g) * u).astype(BF16)
    o_ref[...] = _dot(a, wd_ref[...])


def shared_expert(hb, wg, wu, wd):
    T, D = hb.shape
    H = wg.shape[1]
    tm = 512 if T % 512 == 0 else T
    return pl.pallas_call(
        _shared_kernel,
        grid=(T // tm,),
        in_specs=[pl.BlockSpec((tm, D), lambda i: (i, 0)),
                  pl.BlockSpec((D, H), lambda i: (0, 0)),
                  pl.BlockSpec((D, H), lambda i: (0, 0)),
                  pl.BlockSpec((H, D), lambda i: (0, 0))],
        out_specs=pl.BlockSpec((tm, D), lambda i: (i, 0)),
        out_shape=jax.ShapeDtypeStruct((T, D), F32),
        compiler_params=_cparams(("arbitrary",)),
        name="shared_expert",
    )(hb, wg.astype(BF16), wu.astype(BF16), wd.astype(BF16))


def _combine_kernel(dest_ref, y_hbm, w_ref, sh_ref, x_ref, mod_ref, g_ref, o_ref, buf, sem, *, tt):
    SL = buf.shape[1] // tt

    for k in range(TOP_K):
        def issue(r, c, k=k):
            src = pl.multiple_of(dest_ref[k, r] * SL, SL)
            dst = pl.multiple_of(r * SL, SL)
            pltpu.make_async_copy(y_hbm.at[pl.ds(src, SL), :], buf.at[k, pl.ds(dst, SL), :], sem).start()
            return c
        lax.fori_loop(0, tt, issue, 0, unroll=8)
    for k in range(TOP_K):
        pltpu.make_async_copy(y_hbm.at[pl.ds(0, tt * SL), :], buf.at[k], sem).wait()
    rows = 32
    for rc in range(tt // rows):
        rs = slice(rc * rows, (rc + 1) * rows)
        w = w_ref[rs, :]
        wk = [jnp.broadcast_to(w[:, k:k + 1], (rows, LANES)) for k in range(TOP_K)]
        ssq = jnp.zeros((rows, 1), F32)
        for s in range(SL):
            cl = slice(s * LANES, (s + 1) * LANES)
            acc = sh_ref[rs, cl]
            for k in range(TOP_K):
                acc = acc + wk[k] * buf[k, pl.ds(rc * rows * SL + s, rows, stride=SL), :]
            xs = x_ref[rs, cl] + mod_ref[0, 5:6, cl] * acc
            o_ref[rs, cl] = xs
            ssq = ssq + jnp.sum(xs * xs, axis=-1, keepdims=True)
        rinv = lax.rsqrt(ssq / (SL * LANES) + EPS)
        o_ref[rs, :] = o_ref[rs, :] * rinv * g_ref[...]


def combine(dest, y, top_w_t, shared, x1, mod, g_final, S):
    T, D = x1.shape
    tt = 128
    per_b = S // tt
    kern = functools.partial(_combine_kernel, tt=tt)
    return pl.pallas_call(
        kern,
        grid=(T // tt,),
        in_specs=[pl.BlockSpec((TOP_K, tt), lambda i: (0, i), memory_space=pltpu.SMEM),
                  pl.BlockSpec(memory_space=pl.ANY),
                  pl.BlockSpec((tt, TOP_K), lambda i: (i, 0)),
                  pl.BlockSpec((tt, D), lambda i: (i, 0)),
                  pl.BlockSpec((tt, D), lambda i: (i, 0)),
                  pl.BlockSpec((1, 6, D), lambda i: (i // per_b, 0, 0)),
                  pl.BlockSpec((1, D), lambda i: (0, 0))],
        out_specs=pl.BlockSpec((tt, D), lambda i: (i, 0)),
        out_shape=jax.ShapeDtypeStruct((T, D), F32),
        scratch_shapes=[pltpu.VMEM((TOP_K, tt * (D // LANES), LANES), F32), pltpu.SemaphoreType.DMA(())],
        compiler_params=_cparams(("arbitrary",)),
        name="combine",
    )(dest, y.reshape(-1, LANES), top_w_t, shared, x1, mod, g_final.reshape(1, D))


def _plan_kernel(idx_ref, pos_ref, cnt_ref, ltri_ref, dest_ref, meta_ref, *, nbp):
    E = N_EXPERTS
    tm = idx_ref.shape[1]
    cnt = cnt_ref[...]
    nblk = jnp.floor((cnt + (DISPATCH_BLOCK - 1)) * (1.0 / DISPATCH_BLOCK))
    nblk_rep = jnp.broadcast_to(nblk, (E, LANES))
    ends = _dot(ltri_ref[...], nblk_rep.astype(BF16))
    starts_col = ((ends - nblk_rep) * float(DISPATCH_BLOCK))[:, 0:1]
    ei = lax.broadcasted_iota(I32, (E, tm), 0)
    idx = idx_ref[...]
    rows = [jnp.sum(jnp.where(ei == idx[k:k + 1, :], starts_col, 0.0), axis=0, keepdims=True)
            for k in range(TOP_K)]
    dest_ref[...] = jnp.concatenate(rows, axis=0).astype(I32) + pos_ref[...]
    bi = lax.broadcasted_iota(I32, (E, nbp), 1).astype(F32)
    be = jnp.sum(jnp.where(ends[:, 0:1] <= bi, 1.0, 0.0), axis=0, keepdims=True)
    be = jnp.minimum(be, float(E - 1))
    bv = jnp.where(bi[0:1, :] < ends[E - 1:E, 0:1], 1.0, 0.0)
    meta_ref[...] = jnp.concatenate([be, bv], axis=0).astype(I32)


def dispatch_plan(top_i, pos, counts, n_blocks):
    K, T = top_i.shape
    E = N_EXPERTS
    tm = 1024 if T % 1024 == 0 else T
    nbp = -(-n_blocks // LANES) * LANES
    ltri = jnp.asarray(np.tril(np.ones((E, E), np.float32)), dtype=BF16)
    dest, meta = pl.pallas_call(
        functools.partial(_plan_kernel, nbp=nbp),
        grid=(T // tm,),
        in_specs=[pl.BlockSpec((K, tm), lambda i: (0, i)),
                  pl.BlockSpec((K, tm), lambda i: (0, i)),
                  pl.BlockSpec((E, 1), lambda i: (0, 0)),
                  pl.BlockSpec((E, E), lambda i: (0, 0))],
        out_specs=[pl.BlockSpec((K, tm), lambda i: (0, i)),
                   pl.BlockSpec((2, nbp), lambda i: (0, 0))],
        out_shape=[jax.ShapeDtypeStruct((K, T), I32), jax.ShapeDtypeStruct((2, nbp), I32)],
        compiler_params=_cparams(("arbitrary",)),
        name="dispatch_plan",
    )(top_i, pos, counts, ltri)
    return dest, meta[0, :n_blocks], meta[1, :n_blocks]


def _rowtok_kernel(dest_ref, o_ref, *, tc, n_rows):
    i = pl.program_id(0)

    @pl.when(i == 0)
    def _():
        def clear(r, c):
            o_ref[r] = 0
            return c
        lax.fori_loop(0, n_rows, clear, 0, unroll=16)

    base = i * tc

    def body(t, c):
        for k in range(TOP_K):
            o_ref[dest_ref[k, t]] = base + t
        return c

    lax.fori_loop(0, tc, body, 0, unroll=2)


def row_tokens(dest, n_rows):
    K, T = dest.shape
    tc = 1024 if T % 1024 == 0 else T
    return pl.pallas_call(
        functools.partial(_rowtok_kernel, tc=tc, n_rows=n_rows),
        grid=(T // tc,),
        in_specs=[pl.BlockSpec((K, tc), lambda i: (0, i), memory_space=pltpu.SMEM)],
        out_specs=pl.BlockSpec(memory_space=pltpu.SMEM),
        out_shape=jax.ShapeDtypeStruct((n_rows,), I32),
        compiler_params=_cparams(("arbitrary",)),
        name="row_tokens",
    )(dest)


def moe_ffn(x1, mod, g_ffn, w_router, router_bias, w_exp_gate, w_exp_up, w_exp_down,
            w_sh_gate, w_sh_up, w_sh_down, g_final, S):
    T, D = x1.shape
    n_blocks = -(-(T * TOP_K + N_EXPERTS * (DISPATCH_BLOCK - 1)) // DISPATCH_BLOCK)
    n_blocks += n_blocks % 2
    h, hb, h3 = ffn_input(x1, g_ffn, mod, S)
    top_i, top_w, pos, counts = router(h, w_router, router_bias)
    dest, block_e, block_valid = dispatch_plan(top_i, pos, counts, n_blocks)
    row_tok = row_tokens(dest, n_blocks * DISPATCH_BLOCK)
    xs = dispatch_rows(h3.reshape(T, D // LANES, LANES), row_tok, 2 * DISPATCH_BLOCK)
    y = grouped_experts(xs, block_e, block_valid, w_exp_gate, w_exp_up, w_exp_down)
    shared = shared_expert(hb, w_sh_gate, w_sh_up, w_sh_down)
    return combine(dest, y, top_w.T, shared, x1, mod, g_final, S)


def kernel(x, c, positions, w_ada, b_ada, g_mix, w_in, cmp_k_pe, cmp_k_w1, cmp_k_w2, cmp_v_pe, cmp_v_w1,
           cmp_v_w2, conv_w, conv_b, conv_ln_g, conv_ln_b, w_out, g_ffn, w_router, router_bias, w_exp_gate,
           w_exp_up, w_exp_down, w_sh_gate, w_sh_up, w_sh_down, g_final):
    B, S, D = x.shape
    assert w_ada.shape[0] == 1
    x2 = x.reshape(B * S, D)
    mod = ada_mod(c, w_ada[0], b_ada[0])
    cs, sn = rope_table(positions)
    proj = in_projection(x2, g_mix[0], mod, _prep_w_in(w_in[0]), cs, sn, S)
    k_c, v_c = compress_kv(proj, cs, sn, B, S, cmp_k_pe[0], cmp_k_w1[0], cmp_k_w2[0],
                           cmp_v_pe[0], cmp_v_w1[0], cmp_v_w2[0])
    o_attn = nsa_attention(proj, k_c, v_c, B, S)
    o_conv = conformer_conv(proj, conv_w[0], conv_b[0], conv_ln_g[0], conv_ln_b[0], B, S)
    x1 = out_projection(o_attn, o_conv, w_out[0], x2, mod, S)
    out = moe_ffn(x1, mod, g_ffn[0], w_router[0], router_bias[0], w_exp_gate[0], w_exp_up[0],
                  w_exp_down[0], w_sh_gate[0], w_sh_up[0], w_sh_down[0], g_final, S)
    return out.reshape(B, S, D)
```

```python
import functools

import numpy as np
import jax
import jax.numpy as jnp
from jax import lax
from jax.experimental import pallas as pl
from jax.experimental.pallas import tpu as pltpu

F32 = jnp.float32
BF16 = jnp.bfloat16
I32 = jnp.int32

HEAD_DIM = 128
N_Q_HEADS = 8
N_KV_GROUPS = 2
HEADS_PER_GROUP = N_Q_HEADS // N_KV_GROUPS
N_BRANCH = 3
CMP_BLOCK = 32
CMP_STRIDE = 16
SEL_BLOCK = 64
SEL_TOPK = 16
WINDOW = 512
ROPE_THETA = 10000.0
CONV_KERNEL = 31
N_EXPERTS = 256
TOP_K = 8
N_EXPERT_GROUPS = 8
TOPK_GROUPS = 4
ROUTE_SCALE = 2.5
DISPATCH_BLOCK = 128
EPS = 1e-6
NEG_INF = -1e30
FORCE = 1e30

LANES = 128
VMEM_LIMIT = 56 * 1024 * 1024

C_GLU = 0
C_Q = 2048
C_KSL = 3072
C_KWN = 3328
C_KC = 3584
C_VC = 3840
C_VSL = 4096
C_VWN = 4352
C_GATE = 4608
IN_PAD = 4864


def _dot(a, b):
    return jnp.dot(a, b, preferred_element_type=F32)


def _dot_nt(a, b):
    return lax.dot_general(a, b, (((1,), (1,)), ((), ())), preferred_element_type=F32)


def _cparams(sem, vmem=VMEM_LIMIT):
    return pltpu.CompilerParams(dimension_semantics=sem, vmem_limit_bytes=vmem)


def _ada_kernel(c_ref, w_ref, b_ref, o_ref):
    c = c_ref[...]
    a = (c * jax.nn.sigmoid(c)).astype(BF16)
    o_ref[...] = _dot(a, w_ref[...].astype(BF16)) + b_ref[...]


def ada_mod(c, w, b):
    B, D = c.shape
    N = w.shape[1]
    tn = 1024
    cp = jnp.pad(c, ((0, 8 - B), (0, 0)))
    out = pl.pallas_call(
        _ada_kernel,
        grid=(N // tn,),
        in_specs=[pl.BlockSpec((8, D), lambda j: (0, 0)),
                  pl.BlockSpec((D, tn), lambda j: (0, j)),
                  pl.BlockSpec((1, tn), lambda j: (0, j))],
        out_specs=pl.BlockSpec((8, tn), lambda j: (0, j)),
        out_shape=jax.ShapeDtypeStruct((8, N), F32),
        compiler_params=_cparams(("arbitrary",)),
        name="ada_mod",
    )(cp, w, b.reshape(1, N))
    return out[:B].reshape(B, 6, D)


def _rope_kernel(pos_ref, inv_ref, sign_ref, cs_ref, sn_ref):
    ang = pos_ref[...].astype(F32) * inv_ref[...]
    cs_ref[...] = jnp.cos(ang)
    sn_ref[...] = jnp.sin(ang) * sign_ref[...]


def rope_table(positions):
    T = positions.size
    half = HEAD_DIM // 2
    inv = ROPE_THETA ** (-jnp.arange(half, dtype=F32) / half)
    inv2 = jnp.concatenate([inv, inv]).reshape(1, HEAD_DIM)
    sign = jnp.concatenate([-jnp.ones((half,), F32), jnp.ones((half,), F32)]).reshape(1, HEAD_DIM)
    tm = min(T, 1024)
    return pl.pallas_call(
        _rope_kernel,
        grid=(T // tm,),
        in_specs=[pl.BlockSpec((tm, 1), lambda i: (i, 0)),
                  pl.BlockSpec((1, HEAD_DIM), lambda i: (0, 0)),
                  pl.BlockSpec((1, HEAD_DIM), lambda i: (0, 0))],
        out_specs=[pl.BlockSpec((tm, HEAD_DIM), lambda i: (i, 0)),
                   pl.BlockSpec((tm, HEAD_DIM), lambda i: (i, 0))],
        out_shape=[jax.ShapeDtypeStruct((T, HEAD_DIM), F32)] * 2,
        compiler_params=_cparams(("arbitrary",)),
        name="rope_table",
    )(positions.reshape(T, 1), inv2, sign)


def _rope_apply(t, cs, sn):
    return t * cs + pltpu.roll(t, HEAD_DIM // 2, 1) * sn


def _prep_w_in(w_in):
    D = w_in.shape[0]
    q = w_in[:, 0:1024]
    kc = w_in[:, 1024:1280]
    vc = w_in[:, 1280:1536]
    ksl = w_in[:, 1536:1792]
    vsl = w_in[:, 1792:2048]
    kwn = w_in[:, 2048:2304]
    vwn = w_in[:, 2304:2560]
    gates = w_in[:, 2560:2584].reshape(D, N_BRANCH, N_KV_GROUPS, HEADS_PER_GROUP)
    glu = w_in[:, 2584:4632]
    gparts = []
    for g in range(N_KV_GROUPS):
        gg = gates[:, :, g, :].reshape(D, N_BRANCH * HEADS_PER_GROUP)
        gparts.append(jnp.pad(gg, ((0, 0), (0, LANES - N_BRANCH * HEADS_PER_GROUP))))
    return jnp.concatenate([glu, q, ksl, kwn, kc, vc, vsl, vwn] + gparts, axis=1).astype(BF16)


def _inproj_kernel(x_ref, g_ref, mod_ref, w_ref, cs_ref, sn_ref, o_ref, hn_ref, *, tn, rope_lo, rope_hi, q_hi):
    j = pl.program_id(1)

    @pl.when(j == 0)
    def _():
        rows = 256
        for r in range(x_ref.shape[0] // rows):
            x = x_ref[r * rows:(r + 1) * rows, :]
            y = x * lax.rsqrt(jnp.mean(x * x, axis=-1, keepdims=True) + EPS) * g_ref[...]
            h = y * (1.0 + mod_ref[0, 1:2, :]) + mod_ref[0, 0:1, :]
            hn_ref[r * rows:(r + 1) * rows, :] = h.astype(BF16)

    acc = _dot(hn_ref[...], w_ref[...])
    is_rope = jnp.logical_and(j >= rope_lo, j < rope_hi)

    @pl.when(is_rope)
    def _():
        qs = jnp.where(j < q_hi, HEAD_DIM ** -0.5, 1.0)
        cs = cs_ref[...] * qs
        sn = sn_ref[...] * qs
        parts = [_rope_apply(acc[:, k * LANES:(k + 1) * LANES], cs, sn) for k in range(tn // LANES)]
        o_ref[...] = jnp.concatenate(parts, axis=1).astype(BF16)

    @pl.when(jnp.logical_not(is_rope))
    def _():
        o_ref[...] = acc.astype(BF16)


def in_projection(x2, g_mix, mod, w_in_p, cs, sn, S):
    T, D = x2.shape
    N = w_in_p.shape[1]
    tm = min(S, 1024)
    tn = 256
    per_b = S // tm
    kern = functools.partial(_inproj_kernel, tn=tn, rope_lo=C_Q // tn, rope_hi=C_KC // tn, q_hi=C_KSL // tn)
    return pl.pallas_call(
        kern,
        grid=(T // tm, N // tn),
        in_specs=[pl.BlockSpec((tm, D), lambda i, j: (i, 0)),
                  pl.BlockSpec((1, D), lambda i, j: (0, 0)),
                  pl.BlockSpec((1, 6, D), lambda i, j: (i // per_b, 0, 0)),
                  pl.BlockSpec((D, tn), lambda i, j: (0, j)),
                  pl.BlockSpec((tm, HEAD_DIM), lambda i, j: (i, 0)),
                  pl.BlockSpec((tm, HEAD_DIM), lambda i, j: (i, 0))],
        out_specs=pl.BlockSpec((tm, tn), lambda i, j: (i, j)),
        out_shape=jax.ShapeDtypeStruct((T, N), BF16),
        scratch_shapes=[pltpu.VMEM((tm, D), BF16)],
        compiler_params=_cparams(("arbitrary", "arbitrary")),
        name="in_projection",
    )(x2, g_mix.reshape(1, D), mod, w_in_p, cs, sn)


def _compress_kernel(z_ref, cs_ref, sn_ref, kpe_ref, kw1_ref, kw2_ref, vpe_ref, vw1_ref, vw2_ref,
                     kc_ref, vc_ref):
    nch = z_ref.shape[1]
    half = CMP_BLOCK // 2 * HEAD_DIM
    cs = pltpu.roll(cs_ref[pl.ds(CMP_STRIDE - 1, nch, stride=CMP_STRIDE), :], nch - 1, 0)
    sn = pltpu.roll(sn_ref[pl.ds(CMP_STRIDE - 1, nch, stride=CMP_STRIDE), :], nch - 1, 0)
    for kv, (pe_ref, w1_ref, w2_ref, out_ref) in enumerate(
            ((kpe_ref, kw1_ref, kw2_ref, kc_ref), (vpe_ref, vw1_ref, vw2_ref, vc_ref))):
        for g in range(N_KV_GROUPS):
            col = kv * 256 + g * HEAD_DIM
            zc = jnp.concatenate(
                [z_ref[0, :, l * 512 + col:l * 512 + col + HEAD_DIM] for l in range(CMP_STRIDE)],
                axis=1).astype(F32)
            a = _dot((zc + pe_ref[:, 0:half]).astype(BF16), w1_ref[0:half, :])
            b = _dot((zc + pe_ref[:, half:2 * half]).astype(BF16), w1_ref[half:2 * half, :])
            flat = a + pltpu.roll(b, nch - 1, 0)
            hid = flat * jax.nn.sigmoid(flat)
            out = _dot(hid.astype(BF16), w2_ref[...])
            if kv == 0:
                out = _rope_apply(out, cs, sn)
            out_ref[0, g] = out.astype(BF16)


def compress_kv(proj, cs, sn, B, S, kpe, kw1, kw2, vpe, vw1, vw2):
    nch = S // CMP_STRIDE
    z = proj[:, C_KC:C_KC + 512].reshape(B, nch, CMP_STRIDE * 512)
    flat_pe = lambda pe: pe.reshape(1, CMP_BLOCK * HEAD_DIM)
    full = lambda a: pl.BlockSpec(a.shape, lambda b: (0,) * a.ndim)
    args = (flat_pe(kpe), kw1.astype(BF16), kw2.astype(BF16), flat_pe(vpe), vw1.astype(BF16), vw2.astype(BF16))
    out_sd = jax.ShapeDtypeStruct((B, N_KV_GROUPS, nch, HEAD_DIM), BF16)
    return pl.pallas_call(
        _compress_kernel,
        grid=(B,),
        in_specs=[pl.BlockSpec((1, nch, CMP_STRIDE * 512), lambda b: (b, 0, 0)),
                  pl.BlockSpec((S, HEAD_DIM), lambda b: (b, 0)),
                  pl.BlockSpec((S, HEAD_DIM), lambda b: (b, 0))] + [full(a) for a in args],
        out_specs=[pl.BlockSpec((1, N_KV_GROUPS, nch, HEAD_DIM), lambda b: (b, 0, 0, 0))] * 2,
        out_shape=[out_sd, out_sd],
        compiler_params=_cparams(("arbitrary",)),
        name="compress_kv",
    )(z, cs, sn, *args)


def _softmax_rows(s, valid):
    m = jnp.max(s, axis=1, keepdims=True)
    p = jnp.where(valid, jnp.exp(s - m), 0.0)
    l = jnp.sum(p, axis=1, keepdims=True)
    return p, l


def _nsa_kernel(q_ref, ksl_ref, vsl_ref, kwn_ref, vwn_ref, kc_ref, vc_ref, gt_ref, ovt_ref, e_ref,
                o_ref, m_scr, l_scr, acc_scr, *, tq, tk, n_sel):
    i = pl.program_id(2)
    t0 = i * tq
    HPG = HEADS_PER_GROUP
    R = HPG * tq
    q = q_ref[...]
    qs = jnp.concatenate([q[:, h * HEAD_DIM:(h + 1) * HEAD_DIM] for h in range(HPG)], axis=0)

    kc = kc_ref[0, 0]
    vc = vc_ref[0, 0]
    nc = kc.shape[0]
    s_c = _dot_nt(qs, kc)
    row_c = t0 + (lax.broadcasted_iota(I32, (R, nc), 0) & (tq - 1))
    blk_end = lax.broadcasted_iota(I32, (R, nc), 1) * CMP_STRIDE + (CMP_BLOCK - 1)
    valid_c = blk_end <= row_c
    p_c, l_c = _softmax_rows(jnp.where(valid_c, s_c, NEG_INF), valid_c)
    p_c = p_c / jnp.where(l_c > 0.0, l_c, 1.0)
    o_c = _dot(p_c.astype(BF16), vc)

    psum = p_c[0:tq]
    for h in range(1, HPG):
        psum = psum + p_c[h * tq:(h + 1) * tq]
    p_hi = psum.astype(BF16)
    p_lo = (psum - p_hi.astype(F32)).astype(BF16)
    ovt = ovt_ref[...]
    imp = _dot_nt(ovt, p_hi) + _dot_nt(ovt, p_lo)
    jb = lax.broadcasted_iota(I32, (n_sel, tq), 0)
    tt = t0 + lax.broadcasted_iota(I32, (n_sel, tq), 1)
    cur = tt // SEL_BLOCK
    forced = (jb == 0) | (jb == cur) | (jb == cur - 1)
    causal_blk = jb * SEL_BLOCK <= tt
    imp = jnp.where(forced, FORCE, jnp.where(causal_blk, imp, NEG_INF))
    rank = jnp.zeros((n_sel, tq), F32)
    for b in range(n_sel):
        rb = imp[b:b + 1, :]
        beats = (rb > imp) | ((rb == imp) & (jb > b))
        rank = rank + jnp.where(beats, 1.0, 0.0)
    selb_t = jnp.where(rank < float(min(SEL_TOPK, n_sel)), 0.0, NEG_INF)
    selb_t = jnp.concatenate([selb_t, jnp.zeros((LANES - n_sel, tq), F32)], axis=0)
    selb = selb_t.T.astype(BF16)
    selb4 = jnp.concatenate([selb] * HPG, axis=0)

    m_scr[...] = jnp.full((R, LANES), NEG_INF, F32)
    l_scr[...] = jnp.zeros((R, LANES), F32)
    acc_scr[...] = jnp.zeros((R, HEAD_DIM), F32)

    def sel_tile(kt, on_diagonal):
        ks = pl.multiple_of(kt * tk, tk)
        k = ksl_ref[pl.ds(ks, tk), :]
        v = vsl_ref[pl.ds(ks, tk), :]
        s = _dot_nt(qs, k) + _dot(selb4, e_ref[kt])
        if on_diagonal:
            diff = lax.broadcasted_iota(I32, (R, tk), 1) - (lax.broadcasted_iota(I32, (R, tk), 0) & (tq - 1))
            s = jnp.where(diff <= t0 - ks, s, NEG_INF)
        m_old = m_scr[...]
        m_new = jnp.maximum(m_old, jnp.max(s, axis=1, keepdims=True))
        alpha = jnp.exp(m_old - m_new)
        p = jnp.exp(s - jnp.concatenate([m_new] * (tk // LANES), axis=1))
        l_scr[...] = alpha * l_scr[...] + jnp.sum(p, axis=1, keepdims=True)
        acc_scr[...] = alpha * acc_scr[...] + _dot(p.astype(BF16), v)
        m_scr[...] = m_new

    n_full = t0 // tk

    def sel_body(kt, carry):
        sel_tile(kt, False)
        return carry

    lax.fori_loop(0, n_full, sel_body, 0)
    sel_tile(n_full, True)

    span = WINDOW + tq
    ksw = pl.multiple_of(jnp.maximum(t0 - WINDOW, 0), tq)
    kw = kwn_ref[pl.ds(ksw, span), :]
    vw = vwn_ref[pl.ds(ksw, span), :]
    s_w = _dot_nt(qs, kw)
    rel = (lax.broadcasted_iota(I32, (R, span), 1) - (lax.broadcasted_iota(I32, (R, span), 0) & (tq - 1))
           + (ksw - t0))
    valid_w = (rel <= 0) & (rel > -WINDOW)
    p_w, l_w = _softmax_rows(jnp.where(valid_w, s_w, NEG_INF), valid_w)
    o_w = _dot(p_w.astype(BF16), vw) / l_w

    o_s = acc_scr[...] / l_scr[...]
    gs = jax.nn.sigmoid(gt_ref[...].astype(F32))
    outs = []
    for h in range(HPG):
        sl = slice(h * tq, (h + 1) * tq)
        outs.append(gs[:, h:h + 1] * o_c[sl] + gs[:, HPG + h:HPG + h + 1] * o_s[sl]
                    + gs[:, 2 * HPG + h:2 * HPG + h + 1] * o_w[sl])
    o_ref[...] = jnp.concatenate(outs, axis=1).astype(BF16)


def nsa_attention(proj, k_c, v_c, B, S):
    T = B * S
    tq = 128
    tk = 256
    nq = S // tq
    n_sel = S // SEL_BLOCK
    nc = S // CMP_STRIDE
    assert S >= WINDOW + tq and n_sel <= LANES and S % tk == 0
    cs_ = np.arange(nc) * CMP_STRIDE
    js_ = np.arange(n_sel) * SEL_BLOCK
    overlap = ((cs_[:, None] < js_[None, :] + SEL_BLOCK) & (cs_[:, None] + CMP_BLOCK > js_[None, :]))
    overlap[nc - CMP_BLOCK // CMP_STRIDE + 1:] = False
    ovt = jnp.asarray(overlap.T.astype(np.float32), dtype=BF16)
    keys = np.arange(S)
    e_np = (keys[None, :] // SEL_BLOCK == np.arange(LANES)[:, None]).astype(np.float32)
    e3 = jnp.asarray(e_np.reshape(LANES, S // tk, tk).transpose(1, 0, 2), dtype=BF16)
    qb = C_Q // 512
    cb = lambda c0: c0 // HEAD_DIM
    kern = functools.partial(_nsa_kernel, tq=tq, tk=tk, n_sel=n_sel)
    R = HEADS_PER_GROUP * tq
    kvspec = lambda c0: pl.BlockSpec((S, HEAD_DIM), lambda b, g, i: (b, cb(c0) + g))
    return pl.pallas_call(
        kern,
        grid=(B, N_KV_GROUPS, nq),
        in_specs=[pl.BlockSpec((tq, 512), lambda b, g, i: (b * nq + i, qb + g)),
                  kvspec(C_KSL), kvspec(C_VSL), kvspec(C_KWN), kvspec(C_VWN),
                  pl.BlockSpec((1, 1, nc, HEAD_DIM), lambda b, g, i: (b, g, 0, 0)),
                  pl.BlockSpec((1, 1, nc, HEAD_DIM), lambda b, g, i: (b, g, 0, 0)),
                  pl.BlockSpec((tq, LANES), lambda b, g, i: (b * nq + i, cb(C_GATE) + g)),
                  pl.BlockSpec((n_sel, nc), lambda b, g, i: (0, 0)),
                  pl.BlockSpec((S // tk, LANES, tk), lambda b, g, i: (0, 0, 0))],
        out_specs=pl.BlockSpec((tq, 512), lambda b, g, i: (b * nq + i, g)),
        out_shape=jax.ShapeDtypeStruct((T, N_Q_HEADS * HEAD_DIM), BF16),
        scratch_shapes=[pltpu.VMEM((R, LANES), F32), pltpu.VMEM((R, LANES), F32),
                        pltpu.VMEM((R, HEAD_DIM), F32)],
        compiler_params=_cparams(("arbitrary", "arbitrary", "arbitrary")),
        name="nsa_attention",
    )(proj, proj, proj, proj, proj, k_c, v_c, proj, ovt, e3)


CONV_HALO = 32


def _conv_kernel(a_ref, b_ref, ha_ref, hb_ref, w_ref, cb_ref, g_ref, be_ref, o_ref, ybuf, cbuf, *, ts, rc):
    i = pl.program_id(1)
    C = a_ref.shape[1]
    a = a_ref[...].astype(F32)
    b = b_ref[...].astype(F32)
    ybuf[CONV_HALO:CONV_HALO + ts, :] = a * jax.nn.sigmoid(b)
    ha = ha_ref[...].astype(F32)
    hb = hb_ref[...].astype(F32)
    ybuf[0:CONV_HALO, :] = jnp.where(i > 0, ha * jax.nn.sigmoid(hb), 0.0)
    off = CONV_HALO - (CONV_KERNEL - 1)
    for r in range(ts // rc):
        for c in range(C // LANES):
            cl = slice(c * LANES, (c + 1) * LANES)
            acc = jnp.broadcast_to(cb_ref[:, cl], (rc, LANES))
            for k in range(CONV_KERNEL):
                acc = acc + w_ref[k:k + 1, cl] * ybuf[r * rc + off + k:r * rc + off + k + rc, cl]
            cbuf[r * rc:(r + 1) * rc, cl] = acc
    y = cbuf[...]
    mu = jnp.mean(y, axis=-1, keepdims=True)
    d = y - mu
    var = jnp.mean(d * d, axis=-1, keepdims=True)
    yn = d * lax.rsqrt(var + EPS) * g_ref[...] + be_ref[...]
    o_ref[...] = (yn * jax.nn.sigmoid(yn)).astype(BF16)


def conformer_conv(proj, conv_w, conv_b, ln_g, ln_b, B, S):
    T = B * S
    C = conv_w.shape[-1]
    ts = 256
    nt = S // ts
    hb_per = ts // CONV_HALO
    kern = functools.partial(_conv_kernel, ts=ts, rc=64)
    halo = lambda cblk: pl.BlockSpec(
        (CONV_HALO, C), lambda b, i: (jnp.maximum((b * nt + i) * hb_per - 1, 0), cblk))
    vec = lambda: pl.BlockSpec((1, C), lambda b, i: (0, 0))
    return pl.pallas_call(
        kern,
        grid=(B, nt),
        in_specs=[pl.BlockSpec((ts, C), lambda b, i: (b * nt + i, 0)),
                  pl.BlockSpec((ts, C), lambda b, i: (b * nt + i, 1)),
                  halo(0), halo(1),
                  pl.BlockSpec((CONV_KERNEL, C), lambda b, i: (0, 0)),
                  vec(), vec(), vec()],
        out_specs=pl.BlockSpec((ts, C), lambda b, i: (b * nt + i, 0)),
        out_shape=jax.ShapeDtypeStruct((T, C), BF16),
        scratch_shapes=[pltpu.VMEM((ts + CONV_HALO, C), F32), pltpu.VMEM((ts, C), F32)],
        compiler_params=_cparams(("arbitrary", "arbitrary")),
        name="conformer_conv",
    )(proj, proj, proj, proj, conv_w.reshape(CONV_KERNEL, C), conv_b.reshape(1, C),
      ln_g.reshape(1, C), ln_b.reshape(1, C))


def _outproj_kernel(oa_ref, oc_ref, w1_ref, w2_ref, x_ref, mod_ref, o_ref):
    acc = _dot(oa_ref[...], w1_ref[...]) + _dot(oc_ref[...], w2_ref[...])
    o_ref[...] = x_ref[...] + mod_ref[0, 2:3, :] * acc


def out_projection(o_attn, o_conv, w_out, x2, mod, S):
    T, D = x2.shape
    K1 = o_attn.shape[1]
    K2 = o_conv.shape[1]
    assert K1 == K2
    tm = min(S, 1024)
    tn = 512
    per_b = S // tm
    wb = w_out.astype(BF16)
    return pl.pallas_call(
        _outproj_kernel,
        grid=(T // tm, D // tn),
        in_specs=[pl.BlockSpec((tm, K1), lambda i, j: (i, 0)),
                  pl.BlockSpec((tm, K2), lambda i, j: (i, 0)),
                  pl.BlockSpec((K1, tn), lambda i, j: (0, j)),
                  pl.BlockSpec((K2, tn), lambda i, j: (1, j)),
                  pl.BlockSpec((tm, tn), lambda i, j: (i, j)),
                  pl.BlockSpec((1, 6, tn), lambda i, j: (i // per_b, 0, j))],
        out_specs=pl.BlockSpec((tm, tn), lambda i, j: (i, j)),
        out_shape=jax.ShapeDtypeStruct((T, D), F32),
        compiler_params=_cparams(("arbitrary", "arbitrary")),
        name="out_projection",
    )(o_attn, o_conv, wb, wb, x2, mod)


def _norm2_kernel(x_ref, g_ref, mod_ref, h_ref, hb_ref, h3_ref):
    x = x_ref[...]
    y = x * lax.rsqrt(jnp.mean(x * x, axis=-1, keepdims=True) + EPS) * g_ref[...]
    h = y * (1.0 + mod_ref[0, 4:5, :]) + mod_ref[0, 3:4, :]
    h_ref[...] = h
    hb_ref[...] = h.astype(BF16)
    SL = h.shape[1] // LANES
    for s in range(SL):
        h3_ref[pl.ds(s, h.shape[0], stride=SL), :] = h[:, s * LANES:(s + 1) * LANES]


def ffn_input(x1, g_ffn, mod, S):
    T, D = x1.shape
    tm = min(S, 512)
    per_b = S // tm
    return pl.pallas_call(
        _norm2_kernel,
        grid=(T // tm,),
        in_specs=[pl.BlockSpec((tm, D), lambda i: (i, 0)),
                  pl.BlockSpec((1, D), lambda i: (0, 0)),
                  pl.BlockSpec((1, 6, D), lambda i: (i // per_b, 0, 0))],
        out_specs=[pl.BlockSpec((tm, D), lambda i: (i, 0)),
                   pl.BlockSpec((tm, D), lambda i: (i, 0)),
                   pl.BlockSpec((tm * (D // LANES), LANES), lambda i: (i, 0))],
        out_shape=[jax.ShapeDtypeStruct((T, D), F32), jax.ShapeDtypeStruct((T, D), BF16),
                   jax.ShapeDtypeStruct((T * (D // LANES), LANES), F32)],
        compiler_params=_cparams(("arbitrary",)),
        name="ffn_input",
    )(x1, g_ffn.reshape(1, D), mod)


def _router_kernel(h_ref, wr_ref, bias_ref, tri_ref, idx_ref, w_ref, pos_ref, cnt_ref, carry):
    i = pl.program_id(0)
    E = N_EXPERTS
    tm = h_ref.shape[0]
    gsz = E // N_EXPERT_GROUPS
    BIG = 3.0e38

    @pl.when(i == 0)
    def _():
        carry[...] = jnp.zeros((E, 1), F32)

    h = h_ref[...]
    h_hi = h.astype(BF16)
    h_lo = (h - h_hi.astype(F32)).astype(BF16)
    wr = wr_ref[...]
    w_hi = wr.astype(BF16)
    w_lo = (wr - w_hi.astype(F32)).astype(BF16)
    logits = _dot_nt(w_hi, h_hi) + (_dot_nt(w_hi, h_lo) + _dot_nt(w_lo, h_hi))
    s = jax.nn.sigmoid(logits)
    sb = s + bias_ref[...]
    gi = lax.broadcasted_iota(I32, (gsz, tm), 0)
    gscores = []
    for g in range(N_EXPERT_GROUPS):
        blk = sb[g * gsz:(g + 1) * gsz, :]
        m1 = jnp.max(blk, axis=0, keepdims=True)
        i1 = jnp.min(jnp.where(blk == m1, gi, gsz), axis=0, keepdims=True)
        m2 = jnp.max(jnp.where(gi == i1, -BIG, blk), axis=0, keepdims=True)
        gscores.append(m1 + m2)
    gsc = jnp.concatenate(gscores, axis=0)
    gj = lax.broadcasted_iota(I32, (N_EXPERT_GROUPS, tm), 0)
    grank = jnp.zeros((N_EXPERT_GROUPS, tm), F32)
    for g in range(N_EXPERT_GROUPS):
        rg = gsc[g:g + 1, :]
        beats = (rg > gsc) | ((rg == gsc) & (gj > g))
        grank = grank + jnp.where(beats, 1.0, 0.0)
    gmask = grank < float(TOPK_GROUPS)
    masked = jnp.concatenate(
        [jnp.where(gmask[g:g + 1, :], sb[g * gsz:(g + 1) * gsz, :], NEG_INF) for g in range(N_EXPERT_GROUPS)],
        axis=0)
    ei = lax.broadcasted_iota(I32, (E, tm), 0)
    idxs, vals = [], []
    onehot_sum = jnp.zeros((E, tm), F32)
    for k in range(TOP_K):
        m = jnp.max(masked, axis=0, keepdims=True)
        ik = jnp.min(jnp.where(masked == m, ei, E), axis=0, keepdims=True)
        hit = ei == ik
        vals.append(jnp.sum(jnp.where(hit, s, 0.0), axis=0, keepdims=True))
        idxs.append(ik)
        onehot_sum = onehot_sum + jnp.where(hit, 1.0, 0.0)
        masked = jnp.where(hit, -BIG, masked)
    top_s = jnp.concatenate(vals, axis=0)
    top_i = jnp.concatenate(idxs, axis=0)
    idx_ref[...] = top_i
    w_ref[...] = top_s / jnp.sum(top_s, axis=0, keepdims=True) * ROUTE_SCALE
    before = _dot(onehot_sum.astype(BF16), tri_ref[...]) + carry[...]
    pos_ref[...] = jnp.concatenate(
        [jnp.sum(jnp.where(ei == idxs[k], before, 0.0), axis=0, keepdims=True) for k in range(TOP_K)],
        axis=0).astype(I32)
    carry[...] = carry[...] + jnp.sum(onehot_sum, axis=1, keepdims=True)
    cnt_ref[...] = carry[...]


def router(h, w_router, router_bias):
    T, D = h.shape
    E = N_EXPERTS
    tm = 512 if T % 512 == 0 else T
    tri = jnp.asarray(np.triu(np.ones((tm, tm), np.float32), k=1), dtype=BF16)
    out = pl.pallas_call(
        _router_kernel,
        grid=(T // tm,),
        in_specs=[pl.BlockSpec((tm, D), lambda i: (i, 0)),
                  pl.BlockSpec((E, D), lambda i: (0, 0)),
                  pl.BlockSpec((E, 1), lambda i: (0, 0)),
                  pl.BlockSpec((tm, tm), lambda i: (0, 0))],
        out_specs=[pl.BlockSpec((TOP_K, tm), lambda i: (0, i)),
                   pl.BlockSpec((TOP_K, tm), lambda i: (0, i)),
                   pl.BlockSpec((TOP_K, tm), lambda i: (0, i)),
                   pl.BlockSpec((E, 1), lambda i: (0, 0))],
        out_shape=[jax.ShapeDtypeStruct((TOP_K, T), I32),
                   jax.ShapeDtypeStruct((TOP_K, T), F32),
                   jax.ShapeDtypeStruct((TOP_K, T), I32),
                   jax.ShapeDtypeStruct((E, 1), F32)],
        scratch_shapes=[pltpu.VMEM((E, 1), F32)],
        compiler_params=_cparams(("arbitrary",)),
        name="router",
    )(h, w_router.T, router_bias.reshape(E, 1), tri)
    return out


def _dispatch_kernel(tok_ref, h_hbm, o_ref, sem, *, rb):
    def issue(r, c):
        tok = tok_ref[0, 0, r]
        pltpu.make_async_copy(h_hbm.at[tok], o_ref.at[r], sem).start()
        return c

    lax.fori_loop(0, rb, issue, 0, unroll=8)
    pltpu.make_async_copy(h_hbm.at[pl.ds(0, rb)], o_ref, sem).wait()


def dispatch_rows(h3, row_tok, rb):
    T, SL, L = h3.shape
    P = row_tok.shape[0]
    nb = P // rb
    kern = functools.partial(_dispatch_kernel, rb=rb)
    return pl.pallas_call(
        kern,
        grid=(nb,),
        in_specs=[pl.BlockSpec((1, 1, rb), lambda i: (i, 0, 0), memory_space=pltpu.SMEM),
                  pl.BlockSpec(memory_space=pl.ANY)],
        out_specs=pl.BlockSpec((rb, SL, L), lambda i: (i, 0, 0)),
        out_shape=jax.ShapeDtypeStruct((P, SL, L), h3.dtype),
        scratch_shapes=[pltpu.SemaphoreType.DMA(())],
        compiler_params=_cparams(("arbitrary",)),
        name="dispatch_rows",
    )(row_tok.reshape(nb, 1, rb), h3)


def _gmm_kernel(be_ref, bv_ref, nx_ref, x_ref, wg_hbm, wu_hbm, wd_hbm, y_ref,
                wgs, wus, wds, wgb, wub, wdb, sems):
    i = pl.program_id(0)
    D = wgb.shape[0]
    SL = D // LANES
    bm = x_ref.shape[0] // SL
    e = be_ref[i]
    fresh = jnp.logical_or(i == 0, e != be_ref[jnp.maximum(i - 1, 0)])

    def weight_copies(ex):
        return (pltpu.make_async_copy(wg_hbm.at[ex], wgs, sems.at[0]),
                pltpu.make_async_copy(wu_hbm.at[ex], wus, sems.at[1]),
                pltpu.make_async_copy(wd_hbm.at[ex], wds, sems.at[2]))

    @pl.when(i == 0)
    def _():
        for cp in weight_copies(e):
            cp.start()

    @pl.when(fresh)
    def _():
        for cp in weight_copies(e):
            cp.wait()
        wgb[...] = wgs[...].astype(BF16)
        wub[...] = wus[...].astype(BF16)
        wdb[...] = wds[...].astype(BF16)
        nxt = nx_ref[i]

        @pl.when(nxt >= 0)
        def _():
            for cp in weight_copies(nxt):
                cp.start()

    @pl.when(bv_ref[i] > 0)
    def _():
        x = jnp.concatenate([x_ref[pl.ds(s, bm, stride=SL), :] for s in range(SL)], axis=1).astype(BF16)
        g = _dot(x, wgb[...])
        u = _dot(x, wub[...])
        a = (g * jax.nn.sigmoid(g) * u).astype(BF16)
        y = _dot(a, wdb[...])
        for s in range(SL):
            y_ref[pl.ds(s, bm, stride=SL), :] = y[:, s * LANES:(s + 1) * LANES]

    @pl.when(bv_ref[i] == 0)
    def _():
        y_ref[...] = jnp.zeros(y_ref.shape, F32)


def grouped_experts(xs, block_e, block_valid, next_e, w_gate, w_up, w_down):
    P, SL, L = xs.shape
    D = SL * L
    H = w_gate.shape[-1]
    bm = DISPATCH_BLOCK
    nb = P // bm
    hbm = pl.BlockSpec(memory_space=pl.ANY)
    gs = pltpu.PrefetchScalarGridSpec(
        num_scalar_prefetch=3,
        grid=(nb,),
        in_specs=[pl.BlockSpec((bm * SL, L), lambda i, be, bv, nx: (i, 0)), hbm, hbm, hbm],
        out_specs=pl.BlockSpec((bm * SL, L), lambda i, be, bv, nx: (i, 0)),
        scratch_shapes=[pltpu.VMEM((D, H), F32), pltpu.VMEM((D, H), F32), pltpu.VMEM((H, D), F32),
                        pltpu.VMEM((D, H), BF16), pltpu.VMEM((D, H), BF16), pltpu.VMEM((H, D), BF16),
                        pltpu.SemaphoreType.DMA((3,))],
    )
    return pl.pallas_call(
        _gmm_kernel,
        grid_spec=gs,
        out_shape=jax.ShapeDtypeStruct((P * SL, L), F32),
        compiler_params=_cparams(("arbitrary",)),
        name="grouped_experts",
    )(block_e, block_valid, next_e, xs.reshape(P * SL, L), w_gate, w_up, w_down).reshape(P, SL, L)


def _shared_kernel(h_ref, wg_ref, wu_ref, wd_ref, o_ref):
    x = h_ref[...]
    g = _dot(x, wg_ref[...])
    u = _dot(x, wu_ref[...])
    a = (g * jax.nn.sigmoid(g) * u).astype(BF16)
    o_ref[...] = _dot(a, wd_ref[...])


def shared_expert(hb, wg, wu, wd):
    T, D = hb.shape
    H = wg.shape[1]
    tm = 512 if T % 512 == 0 else T
    return pl.pallas_call(
        _shared_kernel,
        grid=(T // tm,),
        in_specs=[pl.BlockSpec((tm, D), lambda i: (i, 0)),
                  pl.BlockSpec((D, H), lambda i: (0, 0)),
                  pl.BlockSpec((D, H), lambda i: (0, 0)),
                  pl.BlockSpec((H, D), lambda i: (0, 0))],
        out_specs=pl.BlockSpec((tm, D), lambda i: (i, 0)),
        out_shape=jax.ShapeDtypeStruct((T, D), F32),
        compiler_params=_cparams(("arbitrary",)),
        name="shared_expert",
    )(hb, wg.astype(BF16), wu.astype(BF16), wd.astype(BF16))


def _combine_kernel(dest_ref, dnext_ref, y_hbm, w_ref, sh_ref, x_ref, mod_ref, g_ref, o_ref, bufs, sems, *, tt):
    SL = bufs.shape[2] // tt
    i = pl.program_id(0)
    slot = i % 2

    def gather(d_ref, to_slot):
        for k in range(TOP_K):
            def issue(r, c, k=k):
                src = pl.multiple_of(d_ref[k, r] * SL, SL)
                dst = pl.multiple_of(r * SL, SL)
                pltpu.make_async_copy(y_hbm.at[pl.ds(src, SL), :], bufs.at[to_slot, k, pl.ds(dst, SL), :],
                                      sems.at[to_slot]).start()
                return c
            lax.fori_loop(0, tt, issue, 0, unroll=8)

    @pl.when(i == 0)
    def _():
        gather(dest_ref, 0)

    @pl.when(i + 1 < pl.num_programs(0))
    def _():
        gather(dnext_ref, 1 - slot)

    for k in range(TOP_K):
        pltpu.make_async_copy(y_hbm.at[pl.ds(0, tt * SL), :], bufs.at[slot, k], sems.at[slot]).wait()
    rows = 32
    for rc in range(tt // rows):
        rs = slice(rc * rows, (rc + 1) * rows)
        w = w_ref[rs, :]
        wk = [jnp.broadcast_to(w[:, k:k + 1], (rows, LANES)) for k in range(TOP_K)]
        ssq = jnp.zeros((rows, 1), F32)
        for s in range(SL):
            cl = slice(s * LANES, (s + 1) * LANES)
            acc = sh_ref[rs, cl]
            for k in range(TOP_K):
                acc = acc + wk[k] * bufs[slot, k, pl.ds(rc * rows * SL + s, rows, stride=SL), :]
            xs = x_ref[rs, cl] + mod_ref[0, 5:6, cl] * acc
            o_ref[rs, cl] = xs
            ssq = ssq + jnp.sum(xs * xs, axis=-1, keepdims=True)
        rinv = lax.rsqrt(ssq / (SL * LANES) + EPS)
        o_ref[rs, :] = o_ref[rs, :] * rinv * g_ref[...]


def combine(dest, y, top_w_t, shared, x1, mod, g_final, S):
    T, D = x1.shape
    tt = 128
    per_b = S // tt
    kern = functools.partial(_combine_kernel, tt=tt)
    nt = T // tt
    return pl.pallas_call(
        kern,
        grid=(nt,),
        in_specs=[pl.BlockSpec((TOP_K, tt), lambda i: (0, i), memory_space=pltpu.SMEM),
                  pl.BlockSpec((TOP_K, tt), lambda i: (0, jnp.minimum(i + 1, nt - 1)), memory_space=pltpu.SMEM),
                  pl.BlockSpec(memory_space=pl.ANY),
                  pl.BlockSpec((tt, TOP_K), lambda i: (i, 0)),
                  pl.BlockSpec((tt, D), lambda i: (i, 0)),
                  pl.BlockSpec((tt, D), lambda i: (i, 0)),
                  pl.BlockSpec((1, 6, D), lambda i: (i // per_b, 0, 0)),
                  pl.BlockSpec((1, D), lambda i: (0, 0))],
        out_specs=pl.BlockSpec((tt, D), lambda i: (i, 0)),
        out_shape=jax.ShapeDtypeStruct((T, D), F32),
        scratch_shapes=[pltpu.VMEM((2, TOP_K, tt * (D // LANES), LANES), F32), pltpu.SemaphoreType.DMA((2,))],
        compiler_params=_cparams(("arbitrary",)),
        name="combine",
    )(dest, dest, y.reshape(-1, LANES), top_w_t, shared, x1, mod, g_final.reshape(1, D))


def _plan_kernel(idx_ref, pos_ref, cnt_ref, ltri_ref, dest_ref, meta_ref, *, nbp, n_blocks):
    E = N_EXPERTS
    tm = idx_ref.shape[1]
    cnt = cnt_ref[...]
    nblk = jnp.floor((cnt + (DISPATCH_BLOCK - 1)) * (1.0 / DISPATCH_BLOCK))
    nblk_rep = jnp.broadcast_to(nblk, (E, LANES))
    ends = _dot(ltri_ref[...], nblk_rep.astype(BF16))
    starts_col = ((ends - nblk_rep) * float(DISPATCH_BLOCK))[:, 0:1]
    ei = lax.broadcasted_iota(I32, (E, tm), 0)
    idx = idx_ref[...]
    rows = [jnp.sum(jnp.where(ei == idx[k:k + 1, :], starts_col, 0.0), axis=0, keepdims=True)
            for k in range(TOP_K)]
    dest_ref[...] = jnp.concatenate(rows, axis=0).astype(I32) + pos_ref[...]
    bi = lax.broadcasted_iota(I32, (E, nbp), 1).astype(F32)
    be = jnp.sum(jnp.where(ends[:, 0:1] <= bi, 1.0, 0.0), axis=0, keepdims=True)
    be = jnp.minimum(be, float(E - 1))
    total = ends[E - 1:E, 0:1]
    bv = jnp.where(bi[0:1, :] < total, 1.0, 0.0)
    ecol = lax.broadcasted_iota(I32, (E, 1), 0).astype(F32)
    active = (nblk > 0.0) | ((ecol == float(E - 1)) & (total < float(n_blocks)))
    cand = jnp.where(active & (ecol > be), ecol, float(E))
    nx = jnp.min(cand, axis=0, keepdims=True)
    nx = jnp.where(nx >= float(E), -1.0, nx)
    meta_ref[...] = jnp.concatenate([be, bv, nx], axis=0).astype(I32)


def dispatch_plan(top_i, pos, counts, n_blocks):
    K, T = top_i.shape
    E = N_EXPERTS
    tm = 1024 if T % 1024 == 0 else T
    nbp = -(-n_blocks // LANES) * LANES
    ltri = jnp.asarray(np.tril(np.ones((E, E), np.float32)), dtype=BF16)
    dest, meta = pl.pallas_call(
        functools.partial(_plan_kernel, nbp=nbp, n_blocks=n_blocks),
        grid=(T // tm,),
        in_specs=[pl.BlockSpec((K, tm), lambda i: (0, i)),
                  pl.BlockSpec((K, tm), lambda i: (0, i)),
                  pl.BlockSpec((E, 1), lambda i: (0, 0)),
                  pl.BlockSpec((E, E), lambda i: (0, 0))],
        out_specs=[pl.BlockSpec((K, tm), lambda i: (0, i)),
                   pl.BlockSpec((3, nbp), lambda i: (0, 0))],
        out_shape=[jax.ShapeDtypeStruct((K, T), I32), jax.ShapeDtypeStruct((3, nbp), I32)],
        compiler_params=_cparams(("arbitrary",)),
        name="dispatch_plan",
    )(top_i, pos, counts, ltri)
    return dest, meta[0, :n_blocks], meta[1, :n_blocks], meta[2, :n_blocks]


def _rowtok_kernel(dest_ref, init_hbm, o_hbm, table, sem, *, tc):
    i = pl.program_id(0)

    @pl.when(i == 0)
    def _():
        cp = pltpu.make_async_copy(init_hbm, table, sem)
        cp.start()
        cp.wait()

    base = i * tc

    def body(t, c):
        for k in range(TOP_K):
            table[dest_ref[k, t]] = base + t
        return c

    lax.fori_loop(0, tc, body, 0, unroll=2)

    @pl.when(i == pl.num_programs(0) - 1)
    def _():
        cp = pltpu.make_async_copy(table, o_hbm, sem)
        cp.start()
        cp.wait()


def row_tokens(dest, n_rows):
    K, T = dest.shape
    tc = 1024 if T % 1024 == 0 else T
    init = jnp.arange(n_rows, dtype=I32) % T
    return pl.pallas_call(
        functools.partial(_rowtok_kernel, tc=tc),
        grid=(T // tc,),
        in_specs=[pl.BlockSpec((K, tc), lambda i: (0, i), memory_space=pltpu.SMEM),
                  pl.BlockSpec(memory_space=pl.ANY)],
        out_specs=pl.BlockSpec(memory_space=pl.ANY),
        out_shape=jax.ShapeDtypeStruct((n_rows,), I32),
        scratch_shapes=[pltpu.SMEM((n_rows,), I32), pltpu.SemaphoreType.DMA(())],
        compiler_params=_cparams(("arbitrary",)),
        name="row_tokens",
    )(dest, init)


def moe_ffn(x1, mod, g_ffn, w_router, router_bias, w_exp_gate, w_exp_up, w_exp_down,
            w_sh_gate, w_sh_up, w_sh_down, g_final, S):
    T, D = x1.shape
    n_blocks = -(-(T * TOP_K + N_EXPERTS * (DISPATCH_BLOCK - 1)) // DISPATCH_BLOCK)
    n_blocks += n_blocks % 2
    h, hb, h3 = ffn_input(x1, g_ffn, mod, S)
    top_i, top_w, pos, counts = router(h, w_router, router_bias)
    dest, block_e, block_valid, next_e = dispatch_plan(top_i, pos, counts, n_blocks)
    row_tok = row_tokens(dest, n_blocks * DISPATCH_BLOCK)
    xs = dispatch_rows(h3.reshape(T, D // LANES, LANES), row_tok, 2 * DISPATCH_BLOCK)
    y = grouped_experts(xs, block_e, block_valid, next_e, w_exp_gate, w_exp_up, w_exp_down)
    shared = shared_expert(hb, w_sh_gate, w_sh_up, w_sh_down)
    return combine(dest, y, top_w.T, shared, x1, mod, g_final, S)


def kernel(x, c, positions, w_ada, b_ada, g_mix, w_in, cmp_k_pe, cmp_k_w1, cmp_k_w2, cmp_v_pe, cmp_v_w1,
           cmp_v_w2, conv_w, conv_b, conv_ln_g, conv_ln_b, w_out, g_ffn, w_router, router_bias, w_exp_gate,
           w_exp_up, w_exp_down, w_sh_gate, w_sh_up, w_sh_down, g_final):
    B, S, D = x.shape
    assert w_ada.shape[0] == 1
    x2 = x.reshape(B * S, D)
    mod = ada_mod(c, w_ada[0], b_ada[0])
    cs, sn = rope_table(positions)
    proj = in_projection(x2, g_mix[0], mod, _prep_w_in(w_in[0]), cs, sn, S)
    k_c, v_c = compress_kv(proj, cs, sn, B, S, cmp_k_pe[0], cmp_k_w1[0], cmp_k_w2[0],
                           cmp_v_pe[0], cmp_v_w1[0], cmp_v_w2[0])
    o_attn = nsa_attention(proj, k_c, v_c, B, S)
    o_conv = conformer_conv(proj, conv_w[0], conv_b[0], conv_ln_g[0], conv_ln_b[0], B, S)
    x1 = out_projection(o_attn, o_conv, w_out[0], x2, mod, S)
    out = moe_ffn(x1, mod, g_ffn[0], w_router[0], router_bias[0], w_exp_gate[0], w_exp_up[0],
                  w_exp_down[0], w_sh_gate[0], w_sh_up[0], w_sh_down[0], g_final, S)
    return out.reshape(B, S, D)
```

```python
import functools

import numpy as np
import jax
import jax.numpy as jnp
from jax import lax
from jax.experimental import pallas as pl
from jax.experimental.pallas import tpu as pltpu

F32 = jnp.float32
BF16 = jnp.bfloat16
I32 = jnp.int32

HEAD_DIM = 128
N_Q_HEADS = 8
N_KV_GROUPS = 2
HEADS_PER_GROUP = N_Q_HEADS // N_KV_GROUPS
N_BRANCH = 3
CMP_BLOCK = 32
CMP_STRIDE = 16
SEL_BLOCK = 64
SEL_TOPK = 16
WINDOW = 512
ROPE_THETA = 10000.0
CONV_KERNEL = 31
N_EXPERTS = 256
TOP_K = 8
N_EXPERT_GROUPS = 8
TOPK_GROUPS = 4
ROUTE_SCALE = 2.5
DISPATCH_BLOCK = 128
EPS = 1e-6
NEG_INF = -1e30
FORCE = 1e30

LANES = 128
VMEM_LIMIT = 56 * 1024 * 1024

C_GLU = 0
C_Q = 2048
C_KSL = 3072
C_KWN = 3328
C_KC = 3584
C_VC = 3840
C_VSL = 4096
C_VWN = 4352
C_GATE = 4608
IN_USED = 4864
IN_TILE = 512
IN_PAD = -(-IN_USED // IN_TILE) * IN_TILE


def _dot(a, b):
    return jnp.dot(a, b, preferred_element_type=F32)


def _dot_nt(a, b):
    return lax.dot_general(a, b, (((1,), (1,)), ((), ())), preferred_element_type=F32)


def _cparams(sem, vmem=VMEM_LIMIT):
    return pltpu.CompilerParams(dimension_semantics=sem, vmem_limit_bytes=vmem)


def _ada_kernel(c_ref, w_ref, b_ref, o_ref):
    c = c_ref[...]
    a = (c * jax.nn.sigmoid(c)).astype(BF16)
    o_ref[...] = _dot(a, w_ref[...].astype(BF16)) + b_ref[...]


def ada_mod(c, w, b):
    B, D = c.shape
    N = w.shape[1]
    tn = 1024
    cp = jnp.pad(c, ((0, 8 - B), (0, 0)))
    out = pl.pallas_call(
        _ada_kernel,
        grid=(N // tn,),
        in_specs=[pl.BlockSpec((8, D), lambda j: (0, 0)),
                  pl.BlockSpec((D, tn), lambda j: (0, j)),
                  pl.BlockSpec((1, tn), lambda j: (0, j))],
        out_specs=pl.BlockSpec((8, tn), lambda j: (0, j)),
        out_shape=jax.ShapeDtypeStruct((8, N), F32),
        compiler_params=_cparams(("arbitrary",)),
        name="ada_mod",
    )(cp, w, b.reshape(1, N))
    return out[:B].reshape(B, 6, D)


def _rope_kernel(pos_ref, inv_ref, sign_ref, cs_ref, sn_ref):
    ang = pos_ref[...].astype(F32) * inv_ref[...]
    cs_ref[...] = jnp.cos(ang)
    sn_ref[...] = jnp.sin(ang) * sign_ref[...]


def rope_table(positions):
    T = positions.size
    half = HEAD_DIM // 2
    inv = ROPE_THETA ** (-jnp.arange(half, dtype=F32) / half)
    inv2 = jnp.concatenate([inv, inv]).reshape(1, HEAD_DIM)
    sign = jnp.concatenate([-jnp.ones((half,), F32), jnp.ones((half,), F32)]).reshape(1, HEAD_DIM)
    tm = min(T, 1024)
    return pl.pallas_call(
        _rope_kernel,
        grid=(T // tm,),
        in_specs=[pl.BlockSpec((tm, 1), lambda i: (i, 0)),
                  pl.BlockSpec((1, HEAD_DIM), lambda i: (0, 0)),
                  pl.BlockSpec((1, HEAD_DIM), lambda i: (0, 0))],
        out_specs=[pl.BlockSpec((tm, HEAD_DIM), lambda i: (i, 0)),
                   pl.BlockSpec((tm, HEAD_DIM), lambda i: (i, 0))],
        out_shape=[jax.ShapeDtypeStruct((T, HEAD_DIM), F32)] * 2,
        compiler_params=_cparams(("arbitrary",)),
        name="rope_table",
    )(positions.reshape(T, 1), inv2, sign)


def _rope_apply(t, cs, sn):
    return t * cs + pltpu.roll(t, HEAD_DIM // 2, 1) * sn


def _prep_w_in(w_in):
    D = w_in.shape[0]
    q = w_in[:, 0:1024]
    kc = w_in[:, 1024:1280]
    vc = w_in[:, 1280:1536]
    ksl = w_in[:, 1536:1792]
    vsl = w_in[:, 1792:2048]
    kwn = w_in[:, 2048:2304]
    vwn = w_in[:, 2304:2560]
    gates = w_in[:, 2560:2584].reshape(D, N_BRANCH, N_KV_GROUPS, HEADS_PER_GROUP)
    glu = w_in[:, 2584:4632]
    gparts = []
    for g in range(N_KV_GROUPS):
        gg = gates[:, :, g, :].reshape(D, N_BRANCH * HEADS_PER_GROUP)
        gparts.append(jnp.pad(gg, ((0, 0), (0, LANES - N_BRANCH * HEADS_PER_GROUP))))
    tail = [jnp.zeros((D, IN_PAD - IN_USED), w_in.dtype)]
    return jnp.concatenate([glu, q, ksl, kwn, kc, vc, vsl, vwn] + gparts + tail, axis=1).astype(BF16)


def _inproj_kernel(x_ref, g_ref, mod_ref, w_ref, cs_ref, sn_ref, o_ref, hn_ref, *, tn, rope_lo, rope_hi, q_hi):
    j = pl.program_id(1)

    @pl.when(j == 0)
    def _():
        rows = 256
        for r in range(x_ref.shape[0] // rows):
            x = x_ref[r * rows:(r + 1) * rows, :]
            y = x * lax.rsqrt(jnp.mean(x * x, axis=-1, keepdims=True) + EPS) * g_ref[...]
            h = y * (1.0 + mod_ref[0, 1:2, :]) + mod_ref[0, 0:1, :]
            hn_ref[r * rows:(r + 1) * rows, :] = h.astype(BF16)

    acc = _dot(hn_ref[...], w_ref[...])
    is_rope = jnp.logical_and(j >= rope_lo, j < rope_hi)

    @pl.when(is_rope)
    def _():
        qs = jnp.where(j < q_hi, HEAD_DIM ** -0.5, 1.0)
        cs = cs_ref[...] * qs
        sn = sn_ref[...] * qs
        parts = [_rope_apply(acc[:, k * LANES:(k + 1) * LANES], cs, sn) for k in range(tn // LANES)]
        o_ref[...] = jnp.concatenate(parts, axis=1).astype(BF16)

    @pl.when(jnp.logical_not(is_rope))
    def _():
        o_ref[...] = acc.astype(BF16)


def in_projection(x2, g_mix, mod, w_in_p, cs, sn, S):
    T, D = x2.shape
    N = w_in_p.shape[1]
    tm = min(S, 1024)
    tn = IN_TILE
    assert C_Q % tn == 0 and C_KSL % tn == 0 and C_KC % tn == 0
    per_b = S // tm
    kern = functools.partial(_inproj_kernel, tn=tn, rope_lo=C_Q // tn, rope_hi=C_KC // tn, q_hi=C_KSL // tn)
    return pl.pallas_call(
        kern,
        grid=(T // tm, N // tn),
        in_specs=[pl.BlockSpec((tm, D), lambda i, j: (i, 0)),
                  pl.BlockSpec((1, D), lambda i, j: (0, 0)),
                  pl.BlockSpec((1, 6, D), lambda i, j: (i // per_b, 0, 0)),
                  pl.BlockSpec((D, tn), lambda i, j: (0, j)),
                  pl.BlockSpec((tm, HEAD_DIM), lambda i, j: (i, 0)),
                  pl.BlockSpec((tm, HEAD_DIM), lambda i, j: (i, 0))],
        out_specs=pl.BlockSpec((tm, tn), lambda i, j: (i, j)),
        out_shape=jax.ShapeDtypeStruct((T, N), BF16),
        scratch_shapes=[pltpu.VMEM((tm, D), BF16)],
        compiler_params=_cparams(("arbitrary", "arbitrary")),
        name="in_projection",
    )(x2, g_mix.reshape(1, D), mod, w_in_p, cs, sn)


def _compress_kernel(z_ref, cs_ref, sn_ref, kpe_ref, kw1_ref, kw2_ref, vpe_ref, vw1_ref, vw2_ref,
                     kc_ref, vc_ref):
    nch = z_ref.shape[1]
    half = CMP_BLOCK // 2 * HEAD_DIM
    cs = pltpu.roll(cs_ref[pl.ds(CMP_STRIDE - 1, nch, stride=CMP_STRIDE), :], nch - 1, 0)
    sn = pltpu.roll(sn_ref[pl.ds(CMP_STRIDE - 1, nch, stride=CMP_STRIDE), :], nch - 1, 0)
    for kv, (pe_ref, w1_ref, w2_ref, out_ref) in enumerate(
            ((kpe_ref, kw1_ref, kw2_ref, kc_ref), (vpe_ref, vw1_ref, vw2_ref, vc_ref))):
        for g in range(N_KV_GROUPS):
            col = kv * 256 + g * HEAD_DIM
            zc = jnp.concatenate(
                [z_ref[0, :, l * 512 + col:l * 512 + col + HEAD_DIM] for l in range(CMP_STRIDE)],
                axis=1).astype(F32)
            a = _dot((zc + pe_ref[:, 0:half]).astype(BF16), w1_ref[0:half, :])
            b = _dot((zc + pe_ref[:, half:2 * half]).astype(BF16), w1_ref[half:2 * half, :])
            flat = a + pltpu.roll(b, nch - 1, 0)
            hid = flat * jax.nn.sigmoid(flat)
            out = _dot(hid.astype(BF16), w2_ref[...])
            if kv == 0:
                out = _rope_apply(out, cs, sn)
            out_ref[0, g] = out.astype(BF16)


def compress_kv(proj, cs, sn, B, S, kpe, kw1, kw2, vpe, vw1, vw2):
    nch = S // CMP_STRIDE
    z = proj[:, C_KC:C_KC + 512].reshape(B, nch, CMP_STRIDE * 512)
    flat_pe = lambda pe: pe.reshape(1, CMP_BLOCK * HEAD_DIM)
    full = lambda a: pl.BlockSpec(a.shape, lambda b: (0,) * a.ndim)
    args = (flat_pe(kpe), kw1.astype(BF16), kw2.astype(BF16), flat_pe(vpe), vw1.astype(BF16), vw2.astype(BF16))
    out_sd = jax.ShapeDtypeStruct((B, N_KV_GROUPS, nch, HEAD_DIM), BF16)
    return pl.pallas_call(
        _compress_kernel,
        grid=(B,),
        in_specs=[pl.BlockSpec((1, nch, CMP_STRIDE * 512), lambda b: (b, 0, 0)),
                  pl.BlockSpec((S, HEAD_DIM), lambda b: (b, 0)),
                  pl.BlockSpec((S, HEAD_DIM), lambda b: (b, 0))] + [full(a) for a in args],
        out_specs=[pl.BlockSpec((1, N_KV_GROUPS, nch, HEAD_DIM), lambda b: (b, 0, 0, 0))] * 2,
        out_shape=[out_sd, out_sd],
        compiler_params=_cparams(("arbitrary",)),
        name="compress_kv",
    )(z, cs, sn, *args)


def _softmax_rows(s, valid):
    m = jnp.max(s, axis=1, keepdims=True)
    p = jnp.where(valid, jnp.exp(s - m), 0.0)
    l = jnp.sum(p, axis=1, keepdims=True)
    return p, l


def _nsa_kernel(q_ref, ksl_ref, vsl_ref, kwn_ref, vwn_ref, kc_ref, vc_ref, gt_ref, ovt_ref, e_ref,
                o_ref, m_scr, l_scr, acc_scr, *, tq, tk, n_sel):
    i = pl.program_id(2)
    t0 = i * tq
    HPG = HEADS_PER_GROUP
    R = HPG * tq
    q = q_ref[...]
    qs = jnp.concatenate([q[:, h * HEAD_DIM:(h + 1) * HEAD_DIM] for h in range(HPG)], axis=0)

    kc = kc_ref[0, 0]
    vc = vc_ref[0, 0]
    nc = kc.shape[0]
    s_c = _dot_nt(qs, kc)
    row_c = t0 + (lax.broadcasted_iota(I32, (R, nc), 0) & (tq - 1))
    blk_end = lax.broadcasted_iota(I32, (R, nc), 1) * CMP_STRIDE + (CMP_BLOCK - 1)
    valid_c = blk_end <= row_c
    p_c, l_c = _softmax_rows(jnp.where(valid_c, s_c, NEG_INF), valid_c)
    p_c = p_c / jnp.where(l_c > 0.0, l_c, 1.0)
    o_c = _dot(p_c.astype(BF16), vc)

    psum = p_c[0:tq]
    for h in range(1, HPG):
        psum = psum + p_c[h * tq:(h + 1) * tq]
    p_hi = psum.astype(BF16)
    p_lo = (psum - p_hi.astype(F32)).astype(BF16)
    ovt = ovt_ref[...]
    imp = _dot_nt(ovt, p_hi) + _dot_nt(ovt, p_lo)
    jb = lax.broadcasted_iota(I32, (n_sel, tq), 0)
    tt = t0 + lax.broadcasted_iota(I32, (n_sel, tq), 1)
    cur = tt // SEL_BLOCK
    forced = (jb == 0) | (jb == cur) | (jb == cur - 1)
    causal_blk = jb * SEL_BLOCK <= tt
    imp = jnp.where(forced, FORCE, jnp.where(causal_blk, imp, NEG_INF))
    rank = jnp.zeros((n_sel, tq), F32)
    for b in range(n_sel):
        rb = imp[b:b + 1, :]
        beats = (rb > imp) | ((rb == imp) & (jb > b))
        rank = rank + jnp.where(beats, 1.0, 0.0)
    selb_t = jnp.where(rank < float(min(SEL_TOPK, n_sel)), 0.0, NEG_INF)
    selb_t = jnp.concatenate([selb_t, jnp.zeros((LANES - n_sel, tq), F32)], axis=0)
    selb = selb_t.T.astype(BF16)
    selb4 = jnp.concatenate([selb] * HPG, axis=0)

    m_scr[...] = jnp.full((R, LANES), NEG_INF, F32)
    l_scr[...] = jnp.zeros((R, LANES), F32)
    acc_scr[...] = jnp.zeros((R, HEAD_DIM), F32)

    def sel_tile(kt, on_diagonal):
        ks = pl.multiple_of(kt * tk, tk)
        k = ksl_ref[pl.ds(ks, tk), :]
        v = vsl_ref[pl.ds(ks, tk), :]
        s = _dot_nt(qs, k) + _dot(selb4, e_ref[kt])
        if on_diagonal:
            diff = lax.broadcasted_iota(I32, (R, tk), 1) - (lax.broadcasted_iota(I32, (R, tk), 0) & (tq - 1))
            s = jnp.where(diff <= t0 - ks, s, NEG_INF)
        m_old = m_scr[...]
        m_new = jnp.maximum(m_old, jnp.max(s, axis=1, keepdims=True))
        alpha = jnp.exp(m_old - m_new)
        p = jnp.exp(s - jnp.concatenate([m_new] * (tk // LANES), axis=1))
        l_scr[...] = alpha * l_scr[...] + jnp.sum(p, axis=1, keepdims=True)
        acc_scr[...] = alpha * acc_scr[...] + _dot(p.astype(BF16), v)
        m_scr[...] = m_new

    n_full = t0 // tk

    def sel_body(kt, carry):
        sel_tile(kt, False)
        return carry

    lax.fori_loop(0, n_full, sel_body, 0)
    sel_tile(n_full, True)

    span = WINDOW + tq
    ksw = pl.multiple_of(jnp.maximum(t0 - WINDOW, 0), tq)
    kw = kwn_ref[pl.ds(ksw, span), :]
    vw = vwn_ref[pl.ds(ksw, span), :]
    s_w = _dot_nt(qs, kw)
    rel = (lax.broadcasted_iota(I32, (R, span), 1) - (lax.broadcasted_iota(I32, (R, span), 0) & (tq - 1))
           + (ksw - t0))
    valid_w = (rel <= 0) & (rel > -WINDOW)
    p_w, l_w = _softmax_rows(jnp.where(valid_w, s_w, NEG_INF), valid_w)
    o_w = _dot(p_w.astype(BF16), vw) / l_w

    o_s = acc_scr[...] / l_scr[...]
    gs = jax.nn.sigmoid(gt_ref[...].astype(F32))
    outs = []
    for h in range(HPG):
        sl = slice(h * tq, (h + 1) * tq)
        outs.append(gs[:, h:h + 1] * o_c[sl] + gs[:, HPG + h:HPG + h + 1] * o_s[sl]
                    + gs[:, 2 * HPG + h:2 * HPG + h + 1] * o_w[sl])
    o_ref[...] = jnp.concatenate(outs, axis=1).astype(BF16)


def nsa_attention(proj, k_c, v_c, B, S):
    T = B * S
    tq = 128
    tk = 256
    nq = S // tq
    n_sel = S // SEL_BLOCK
    nc = S // CMP_STRIDE
    assert S >= WINDOW + tq and n_sel <= LANES and S % tk == 0
    cs_ = np.arange(nc) * CMP_STRIDE
    js_ = np.arange(n_sel) * SEL_BLOCK
    overlap = ((cs_[:, None] < js_[None, :] + SEL_BLOCK) & (cs_[:, None] + CMP_BLOCK > js_[None, :]))
    overlap[nc - CMP_BLOCK // CMP_STRIDE + 1:] = False
    ovt = jnp.asarray(overlap.T.astype(np.float32), dtype=BF16)
    keys = np.arange(S)
    e_np = (keys[None, :] // SEL_BLOCK == np.arange(LANES)[:, None]).astype(np.float32)
    e3 = jnp.asarray(e_np.reshape(LANES, S // tk, tk).transpose(1, 0, 2), dtype=BF16)
    qb = C_Q // 512
    cb = lambda c0: c0 // HEAD_DIM
    kern = functools.partial(_nsa_kernel, tq=tq, tk=tk, n_sel=n_sel)
    R = HEADS_PER_GROUP * tq
    kvspec = lambda c0: pl.BlockSpec((S, HEAD_DIM), lambda b, g, i: (b, cb(c0) + g))
    return pl.pallas_call(
        kern,
        grid=(B, N_KV_GROUPS, nq),
        in_specs=[pl.BlockSpec((tq, 512), lambda b, g, i: (b * nq + i, qb + g)),
                  kvspec(C_KSL), kvspec(C_VSL), kvspec(C_KWN), kvspec(C_VWN),
                  pl.BlockSpec((1, 1, nc, HEAD_DIM), lambda b, g, i: (b, g, 0, 0)),
                  pl.BlockSpec((1, 1, nc, HEAD_DIM), lambda b, g, i: (b, g, 0, 0)),
                  pl.BlockSpec((tq, LANES), lambda b, g, i: (b * nq + i, cb(C_GATE) + g)),
                  pl.BlockSpec((n_sel, nc), lambda b, g, i: (0, 0)),
                  pl.BlockSpec((S // tk, LANES, tk), lambda b, g, i: (0, 0, 0))],
        out_specs=pl.BlockSpec((tq, 512), lambda b, g, i: (b * nq + i, g)),
        out_shape=jax.ShapeDtypeStruct((T, N_Q_HEADS * HEAD_DIM), BF16),
        scratch_shapes=[pltpu.VMEM((R, LANES), F32), pltpu.VMEM((R, LANES), F32),
                        pltpu.VMEM((R, HEAD_DIM), F32)],
        compiler_params=_cparams(("arbitrary", "arbitrary", "arbitrary")),
        name="nsa_attention",
    )(proj, proj, proj, proj, proj, k_c, v_c, proj, ovt, e3)


CONV_HALO = 32


def _conv_kernel(a_ref, b_ref, ha_ref, hb_ref, w_ref, cb_ref, g_ref, be_ref, o_ref, ybuf, cbuf, *, ts, rc):
    i = pl.program_id(1)
    C = a_ref.shape[1]
    a = a_ref[...].astype(F32)
    b = b_ref[...].astype(F32)
    ybuf[CONV_HALO:CONV_HALO + ts, :] = a * jax.nn.sigmoid(b)
    ha = ha_ref[...].astype(F32)
    hb = hb_ref[...].astype(F32)
    ybuf[0:CONV_HALO, :] = jnp.where(i > 0, ha * jax.nn.sigmoid(hb), 0.0)
    off = CONV_HALO - (CONV_KERNEL - 1)
    for r in range(ts // rc):
        for c in range(C // LANES):
            cl = slice(c * LANES, (c + 1) * LANES)
            acc = jnp.broadcast_to(cb_ref[:, cl], (rc, LANES))
            for k in range(CONV_KERNEL):
                acc = acc + w_ref[k:k + 1, cl] * ybuf[r * rc + off + k:r * rc + off + k + rc, cl]
            cbuf[r * rc:(r + 1) * rc, cl] = acc
    y = cbuf[...]
    mu = jnp.mean(y, axis=-1, keepdims=True)
    d = y - mu
    var = jnp.mean(d * d, axis=-1, keepdims=True)
    yn = d * lax.rsqrt(var + EPS) * g_ref[...] + be_ref[...]
    o_ref[...] = (yn * jax.nn.sigmoid(yn)).astype(BF16)


def conformer_conv(proj, conv_w, conv_b, ln_g, ln_b, B, S):
    T = B * S
    C = conv_w.shape[-1]
    ts = 256
    nt = S // ts
    hb_per = ts // CONV_HALO
    kern = functools.partial(_conv_kernel, ts=ts, rc=64)
    halo = lambda cblk: pl.BlockSpec(
        (CONV_HALO, C), lambda b, i: (jnp.maximum((b * nt + i) * hb_per - 1, 0), cblk))
    vec = lambda: pl.BlockSpec((1, C), lambda b, i: (0, 0))
    return pl.pallas_call(
        kern,
        grid=(B, nt),
        in_specs=[pl.BlockSpec((ts, C), lambda b, i: (b * nt + i, 0)),
                  pl.BlockSpec((ts, C), lambda b, i: (b * nt + i, 1)),
                  halo(0), halo(1),
                  pl.BlockSpec((CONV_KERNEL, C), lambda b, i: (0, 0)),
                  vec(), vec(), vec()],
        out_specs=pl.BlockSpec((ts, C), lambda b, i: (b * nt + i, 0)),
        out_shape=jax.ShapeDtypeStruct((T, C), BF16),
        scratch_shapes=[pltpu.VMEM((ts + CONV_HALO, C), F32), pltpu.VMEM((ts, C), F32)],
        compiler_params=_cparams(("arbitrary", "arbitrary")),
        name="conformer_conv",
    )(proj, proj, proj, proj, conv_w.reshape(CONV_KERNEL, C), conv_b.reshape(1, C),
      ln_g.reshape(1, C), ln_b.reshape(1, C))


def _outproj_kernel(oa_ref, oc_ref, w1_ref, w2_ref, x_ref, mod_ref, o_ref):
    acc = _dot(oa_ref[...], w1_ref[...]) + _dot(oc_ref[...], w2_ref[...])
    o_ref[...] = x_ref[...] + mod_ref[0, 2:3, :] * acc


def out_projection(o_attn, o_conv, w_out, x2, mod, S):
    T, D = x2.shape
    K1 = o_attn.shape[1]
    K2 = o_conv.shape[1]
    assert K1 == K2
    tm = min(S, 1024)
    tn = 512
    per_b = S // tm
    wb = w_out.astype(BF16)
    return pl.pallas_call(
        _outproj_kernel,
        grid=(T // tm, D // tn),
        in_specs=[pl.BlockSpec((tm, K1), lambda i, j: (i, 0)),
                  pl.BlockSpec((tm, K2), lambda i, j: (i, 0)),
                  pl.BlockSpec((K1, tn), lambda i, j: (0, j)),
                  pl.BlockSpec((K2, tn), lambda i, j: (1, j)),
                  pl.BlockSpec((tm, tn), lambda i, j: (i, j)),
                  pl.BlockSpec((1, 6, tn), lambda i, j: (i // per_b, 0, j))],
        out_specs=pl.BlockSpec((tm, tn), lambda i, j: (i, j)),
        out_shape=jax.ShapeDtypeStruct((T, D), F32),
        compiler_params=_cparams(("arbitrary", "arbitrary")),
        name="out_projection",
    )(o_attn, o_conv, wb, wb, x2, mod)


def _norm2_kernel(x_ref, g_ref, mod_ref, h_ref, hb_ref, h3_ref):
    x = x_ref[...]
    y = x * lax.rsqrt(jnp.mean(x * x, axis=-1, keepdims=True) + EPS) * g_ref[...]
    h = y * (1.0 + mod_ref[0, 4:5, :]) + mod_ref[0, 3:4, :]
    h_ref[...] = h
    hb_ref[...] = h.astype(BF16)
    SL = h.shape[1] // LANES
    for s in range(SL):
        h3_ref[pl.ds(s, h.shape[0], stride=SL), :] = h[:, s * LANES:(s + 1) * LANES]


def ffn_input(x1, g_ffn, mod, S):
    T, D = x1.shape
    tm = min(S, 512)
    per_b = S // tm
    return pl.pallas_call(
        _norm2_kernel,
        grid=(T // tm,),
        in_specs=[pl.BlockSpec((tm, D), lambda i: (i, 0)),
                  pl.BlockSpec((1, D), lambda i: (0, 0)),
                  pl.BlockSpec((1, 6, D), lambda i: (i // per_b, 0, 0))],
        out_specs=[pl.BlockSpec((tm, D), lambda i: (i, 0)),
                   pl.BlockSpec((tm, D), lambda i: (i, 0)),
                   pl.BlockSpec((tm * (D // LANES), LANES), lambda i: (i, 0))],
        out_shape=[jax.ShapeDtypeStruct((T, D), F32), jax.ShapeDtypeStruct((T, D), BF16),
                   jax.ShapeDtypeStruct((T * (D // LANES), LANES), F32)],
        compiler_params=_cparams(("arbitrary",)),
        name="ffn_input",
    )(x1, g_ffn.reshape(1, D), mod)


def _router_kernel(h_ref, wr_ref, bias_ref, tri_ref, idx_ref, w_ref, pos_ref, cnt_ref, carry):
    i = pl.program_id(0)
    E = N_EXPERTS
    tm = h_ref.shape[0]
    gsz = E // N_EXPERT_GROUPS
    BIG = 3.0e38

    @pl.when(i == 0)
    def _():
        carry[...] = jnp.zeros((E, 1), F32)

    h = h_ref[...]
    h_hi = h.astype(BF16)
    h_lo = (h - h_hi.astype(F32)).astype(BF16)
    wr = wr_ref[...]
    w_hi = wr.astype(BF16)
    w_lo = (wr - w_hi.astype(F32)).astype(BF16)
    logits = _dot_nt(w_hi, h_hi) + (_dot_nt(w_hi, h_lo) + _dot_nt(w_lo, h_hi))
    s = jax.nn.sigmoid(logits)
    sb = s + bias_ref[...]
    gi = lax.broadcasted_iota(I32, (gsz, tm), 0)
    gscores = []
    for g in range(N_EXPERT_GROUPS):
        blk = sb[g * gsz:(g + 1) * gsz, :]
        m1 = jnp.max(blk, axis=0, keepdims=True)
        i1 = jnp.min(jnp.where(blk == m1, gi, gsz), axis=0, keepdims=True)
        m2 = jnp.max(jnp.where(gi == i1, -BIG, blk), axis=0, keepdims=True)
        gscores.append(m1 + m2)
    gsc = jnp.concatenate(gscores, axis=0)
    gj = lax.broadcasted_iota(I32, (N_EXPERT_GROUPS, tm), 0)
    grank = jnp.zeros((N_EXPERT_GROUPS, tm), F32)
    for g in range(N_EXPERT_GROUPS):
        rg = gsc[g:g + 1, :]
        beats = (rg > gsc) | ((rg == gsc) & (gj > g))
        grank = grank + jnp.where(beats, 1.0, 0.0)
    gmask = grank < float(TOPK_GROUPS)
    masked = jnp.concatenate(
        [jnp.where(gmask[g:g + 1, :], sb[g * gsz:(g + 1) * gsz, :], NEG_INF) for g in range(N_EXPERT_GROUPS)],
        axis=0)
    ei = lax.broadcasted_iota(I32, (E, tm), 0)
    idxs, vals = [], []
    onehot_sum = jnp.zeros((E, tm), F32)
    for k in range(TOP_K):
        m = jnp.max(masked, axis=0, keepdims=True)
        ik = jnp.min(jnp.where(masked == m, ei, E), axis=0, keepdims=True)
        hit = ei == ik
        vals.append(jnp.sum(jnp.where(hit, s, 0.0), axis=0, keepdims=True))
        idxs.append(ik)
        onehot_sum = onehot_sum + jnp.where(hit, 1.0, 0.0)
        masked = jnp.where(hit, -BIG, masked)
    top_s = jnp.concatenate(vals, axis=0)
    top_i = jnp.concatenate(idxs, axis=0)
    idx_ref[...] = top_i
    w_ref[...] = top_s / jnp.sum(top_s, axis=0, keepdims=True) * ROUTE_SCALE
    before = _dot(onehot_sum.astype(BF16), tri_ref[...]) + carry[...]
    pos_ref[...] = jnp.concatenate(
        [jnp.sum(jnp.where(ei == idxs[k], before, 0.0), axis=0, keepdims=True) for k in range(TOP_K)],
        axis=0).astype(I32)
    carry[...] = carry[...] + jnp.sum(onehot_sum, axis=1, keepdims=True)
    cnt_ref[...] = carry[...]


def router(h, w_router, router_bias):
    T, D = h.shape
    E = N_EXPERTS
    tm = 512 if T % 512 == 0 else T
    tri = jnp.asarray(np.triu(np.ones((tm, tm), np.float32), k=1), dtype=BF16)
    out = pl.pallas_call(
        _router_kernel,
        grid=(T // tm,),
        in_specs=[pl.BlockSpec((tm, D), lambda i: (i, 0)),
                  pl.BlockSpec((E, D), lambda i: (0, 0)),
                  pl.BlockSpec((E, 1), lambda i: (0, 0)),
                  pl.BlockSpec((tm, tm), lambda i: (0, 0))],
        out_specs=[pl.BlockSpec((TOP_K, tm), lambda i: (0, i)),
                   pl.BlockSpec((TOP_K, tm), lambda i: (0, i)),
                   pl.BlockSpec((TOP_K, tm), lambda i: (0, i)),
                   pl.BlockSpec((E, 1), lambda i: (0, 0))],
        out_shape=[jax.ShapeDtypeStruct((TOP_K, T), I32),
                   jax.ShapeDtypeStruct((TOP_K, T), F32),
                   jax.ShapeDtypeStruct((TOP_K, T), I32),
                   jax.ShapeDtypeStruct((E, 1), F32)],
        scratch_shapes=[pltpu.VMEM((E, 1), F32)],
        compiler_params=_cparams(("arbitrary",)),
        name="router",
    )(h, w_router.T, router_bias.reshape(E, 1), tri)
    return out


def _gmm_kernel(be_ref, bv_ref, nx_ref, tok_ref, tokn_ref, h_hbm, wg_hbm, wu_hbm, wd_hbm, y_ref,
                xbuf, wgs, wus, wds, wgb, wub, wdb, sems, xsems):
    i = pl.program_id(0)
    nb = pl.num_programs(0)
    D = wgb.shape[0]
    SL = D // LANES
    bm = xbuf.shape[1] // SL
    slot = i % 2
    e = be_ref[i]
    fresh = jnp.logical_or(i == 0, e != be_ref[jnp.maximum(i - 1, 0)])

    def gather(t_ref, to_slot):
        def issue(r, c):
            src = pl.multiple_of(t_ref[0, 0, r] * SL, SL)
            dst = pl.multiple_of(r * SL, SL)
            pltpu.make_async_copy(h_hbm.at[pl.ds(src, SL), :], xbuf.at[to_slot, pl.ds(dst, SL), :],
                                  xsems.at[to_slot]).start()
            return c
        lax.fori_loop(0, bm, issue, 0, unroll=8)

    @pl.when(jnp.logical_and(i == 0, bv_ref[0] > 0))
    def _():
        gather(tok_ref, 0)

    @pl.when(jnp.logical_and(i + 1 < nb, bv_ref[jnp.minimum(i + 1, nb - 1)] > 0))
    def _():
        gather(tokn_ref, 1 - slot)

    def weight_copies(ex):
        return (pltpu.make_async_copy(wg_hbm.at[ex], wgs, sems.at[0]),
                pltpu.make_async_copy(wu_hbm.at[ex], wus, sems.at[1]),
                pltpu.make_async_copy(wd_hbm.at[ex], wds, sems.at[2]))

    @pl.when(i == 0)
    def _():
        for cp in weight_copies(e):
            cp.start()

    @pl.when(fresh)
    def _():
        for cp in weight_copies(e):
            cp.wait()
        wgb[...] = wgs[...].astype(BF16)
        wub[...] = wus[...].astype(BF16)
        wdb[...] = wds[...].astype(BF16)
        nxt = nx_ref[i]

        @pl.when(nxt >= 0)
        def _():
            for cp in weight_copies(nxt):
                cp.start()

    @pl.when(bv_ref[i] > 0)
    def _():
        pltpu.make_async_copy(h_hbm.at[pl.ds(0, bm * SL), :], xbuf.at[slot], xsems.at[slot]).wait()
        x = jnp.concatenate([xbuf[slot, pl.ds(s, bm, stride=SL), :] for s in range(SL)],
                            axis=1).astype(BF16)
        g = _dot(x, wgb[...])
        u = _dot(x, wub[...])
        a = (g * jax.nn.sigmoid(g) * u).astype(BF16)
        y = _dot(a, wdb[...])
        for s in range(SL):
            y_ref[pl.ds(s, bm, stride=SL), :] = y[:, s * LANES:(s + 1) * LANES]

    @pl.when(bv_ref[i] == 0)
    def _():
        y_ref[...] = jnp.zeros(y_ref.shape, F32)


def grouped_experts(h3, row_tok, block_e, block_valid, next_e, w_gate, w_up, w_down):
    L = h3.shape[1]
    D, H = w_gate.shape[1:]
    SL = D // L
    bm = DISPATCH_BLOCK
    P = row_tok.shape[0]
    nb = P // bm
    hbm = pl.BlockSpec(memory_space=pl.ANY)
    gs = pltpu.PrefetchScalarGridSpec(
        num_scalar_prefetch=3,
        grid=(nb,),
        in_specs=[pl.BlockSpec((1, 1, bm), lambda i, be, bv, nx: (i, 0, 0), memory_space=pltpu.SMEM),
                  pl.BlockSpec((1, 1, bm), lambda i, be, bv, nx: (jnp.minimum(i + 1, nb - 1), 0, 0),
                               memory_space=pltpu.SMEM),
                  hbm, hbm, hbm, hbm],
        out_specs=pl.BlockSpec((bm * SL, L), lambda i, be, bv, nx: (i, 0)),
        scratch_shapes=[pltpu.VMEM((2, bm * SL, L), F32),
                        pltpu.VMEM((D, H), F32), pltpu.VMEM((D, H), F32), pltpu.VMEM((H, D), F32),
                        pltpu.VMEM((D, H), BF16), pltpu.VMEM((D, H), BF16), pltpu.VMEM((H, D), BF16),
                        pltpu.SemaphoreType.DMA((3,)), pltpu.SemaphoreType.DMA((2,))],
    )
    tok3 = row_tok.reshape(nb, 1, bm)
    return pl.pallas_call(
        _gmm_kernel,
        grid_spec=gs,
        out_shape=jax.ShapeDtypeStruct((P * SL, L), F32),
        compiler_params=_cparams(("arbitrary",)),
        name="grouped_experts",
    )(block_e, block_valid, next_e, tok3, tok3, h3, w_gate, w_up, w_down)


def _shared_kernel(h_ref, wg_ref, wu_ref, wd_ref, o_ref):
    x = h_ref[...]
    g = _dot(x, wg_ref[...])
    u = _dot(x, wu_ref[...])
    a = (g * jax.nn.sigmoid(g) * u).astype(BF16)
    o_ref[...] = _dot(a, wd_ref[...])


def shared_expert(hb, wg, wu, wd):
    T, D = hb.shape
    H = wg.shape[1]
    tm = 512 if T % 512 == 0 else T
    return pl.pallas_call(
        _shared_kernel,
        grid=(T // tm,),
        in_specs=[pl.BlockSpec((tm, D), lambda i: (i, 0)),
                  pl.BlockSpec((D, H), lambda i: (0, 0)),
                  pl.BlockSpec((D, H), lambda i: (0, 0)),
                  pl.BlockSpec((H, D), lambda i: (0, 0))],
        out_specs=pl.BlockSpec((tm, D), lambda i: (i, 0)),
        out_shape=jax.ShapeDtypeStruct((T, D), F32),
        compiler_params=_cparams(("arbitrary",)),
        name="shared_expert",
    )(hb, wg.astype(BF16), wu.astype(BF16), wd.astype(BF16))


def _combine_kernel(dest_ref, dnext_ref, y_hbm, w_ref, sh_ref, x_ref, mod_ref, g_ref, o_ref, bufs, sems, *, tt):
    SL = bufs.shape[2] // tt
    i = pl.program_id(0)
    slot = i % 2

    def gather(d_ref, to_slot):
        for k in range(TOP_K):
            def issue(r, c, k=k):
                src = pl.multiple_of(d_ref[k, r] * SL, SL)
                dst = pl.multiple_of(r * SL, SL)
                pltpu.make_async_copy(y_hbm.at[pl.ds(src, SL), :], bufs.at[to_slot, k, pl.ds(dst, SL), :],
                                      sems.at[to_slot]).start()
                return c
            lax.fori_loop(0, tt, issue, 0, unroll=8)

    @pl.when(i == 0)
    def _():
        gather(dest_ref, 0)

    @pl.when(i + 1 < pl.num_programs(0))
    def _():
        gather(dnext_ref, 1 - slot)

    for k in range(TOP_K):
        pltpu.make_async_copy(y_hbm.at[pl.ds(0, tt * SL), :], bufs.at[slot, k], sems.at[slot]).wait()
    rows = 32
    for rc in range(tt // rows):
        rs = slice(rc * rows, (rc + 1) * rows)
        w = w_ref[rs, :]
        wk = [jnp.broadcast_to(w[:, k:k + 1], (rows, LANES)) for k in range(TOP_K)]
        ssq = jnp.zeros((rows, 1), F32)
        for s in range(SL):
            cl = slice(s * LANES, (s + 1) * LANES)
            acc = sh_ref[rs, cl]
            for k in range(TOP_K):
                acc = acc + wk[k] * bufs[slot, k, pl.ds(rc * rows * SL + s, rows, stride=SL), :]
            xs = x_ref[rs, cl] + mod_ref[0, 5:6, cl] * acc
            o_ref[rs, cl] = xs
            ssq = ssq + jnp.sum(xs * xs, axis=-1, keepdims=True)
        rinv = lax.rsqrt(ssq / (SL * LANES) + EPS)
        o_ref[rs, :] = o_ref[rs, :] * rinv * g_ref[...]


def combine(dest, y, top_w_t, shared, x1, mod, g_final, S):
    T, D = x1.shape
    tt = 128
    per_b = S // tt
    kern = functools.partial(_combine_kernel, tt=tt)
    nt = T // tt
    return pl.pallas_call(
        kern,
        grid=(nt,),
        in_specs=[pl.BlockSpec((TOP_K, tt), lambda i: (0, i), memory_space=pltpu.SMEM),
                  pl.BlockSpec((TOP_K, tt), lambda i: (0, jnp.minimum(i + 1, nt - 1)), memory_space=pltpu.SMEM),
                  pl.BlockSpec(memory_space=pl.ANY),
                  pl.BlockSpec((tt, TOP_K), lambda i: (i, 0)),
                  pl.BlockSpec((tt, D), lambda i: (i, 0)),
                  pl.BlockSpec((tt, D), lambda i: (i, 0)),
                  pl.BlockSpec((1, 6, D), lambda i: (i // per_b, 0, 0)),
                  pl.BlockSpec((1, D), lambda i: (0, 0))],
        out_specs=pl.BlockSpec((tt, D), lambda i: (i, 0)),
        out_shape=jax.ShapeDtypeStruct((T, D), F32),
        scratch_shapes=[pltpu.VMEM((2, TOP_K, tt * (D // LANES), LANES), F32), pltpu.SemaphoreType.DMA((2,))],
        compiler_params=_cparams(("arbitrary",)),
        name="combine",
    )(dest, dest, y.reshape(-1, LANES), top_w_t, shared, x1, mod, g_final.reshape(1, D))


def _plan_kernel(idx_ref, pos_ref, cnt_ref, ltri_ref, dest_ref, meta_ref, *, nbp, n_blocks):
    E = N_EXPERTS
    tm = idx_ref.shape[1]
    cnt = cnt_ref[...]
    nblk = jnp.floor((cnt + (DISPATCH_BLOCK - 1)) * (1.0 / DISPATCH_BLOCK))
    nblk_rep = jnp.broadcast_to(nblk, (E, LANES))
    ends = _dot(ltri_ref[...], nblk_rep.astype(BF16))
    starts_col = ((ends - nblk_rep) * float(DISPATCH_BLOCK))[:, 0:1]
    ei = lax.broadcasted_iota(I32, (E, tm), 0)
    idx = idx_ref[...]
    rows = [jnp.sum(jnp.where(ei == idx[k:k + 1, :], starts_col, 0.0), axis=0, keepdims=True)
            for k in range(TOP_K)]
    dest_ref[...] = jnp.concatenate(rows, axis=0).astype(I32) + pos_ref[...]
    bi = lax.broadcasted_iota(I32, (E, nbp), 1).astype(F32)
    be = jnp.sum(jnp.where(ends[:, 0:1] <= bi, 1.0, 0.0), axis=0, keepdims=True)
    be = jnp.minimum(be, float(E - 1))
    total = ends[E - 1:E, 0:1]
    bv = jnp.where(bi[0:1, :] < total, 1.0, 0.0)
    ecol = lax.broadcasted_iota(I32, (E, 1), 0).astype(F32)
    active = (nblk > 0.0) | ((ecol == float(E - 1)) & (total < float(n_blocks)))
    cand = jnp.where(active & (ecol > be), ecol, float(E))
    nx = jnp.min(cand, axis=0, keepdims=True)
    nx = jnp.where(nx >= float(E), -1.0, nx)
    meta_ref[...] = jnp.concatenate([be, bv, nx], axis=0).astype(I32)


def dispatch_plan(top_i, pos, counts, n_blocks):
    K, T = top_i.shape
    E = N_EXPERTS
    tm = 1024 if T % 1024 == 0 else T
    nbp = -(-n_blocks // LANES) * LANES
    ltri = jnp.asarray(np.tril(np.ones((E, E), np.float32)), dtype=BF16)
    dest, meta = pl.pallas_call(
        functools.partial(_plan_kernel, nbp=nbp, n_blocks=n_blocks),
        grid=(T // tm,),
        in_specs=[pl.BlockSpec((K, tm), lambda i: (0, i)),
                  pl.BlockSpec((K, tm), lambda i: (0, i)),
                  pl.BlockSpec((E, 1), lambda i: (0, 0)),
                  pl.BlockSpec((E, E), lambda i: (0, 0))],
        out_specs=[pl.BlockSpec((K, tm), lambda i: (0, i)),
                   pl.BlockSpec((3, nbp), lambda i: (0, 0))],
        out_shape=[jax.ShapeDtypeStruct((K, T), I32), jax.ShapeDtypeStruct((3, nbp), I32)],
        compiler_params=_cparams(("arbitrary",)),
        name="dispatch_plan",
    )(top_i, pos, counts, ltri)
    return dest, meta[0, :n_blocks], meta[1, :n_blocks], meta[2, :n_blocks]


def _rowtok_kernel(dest_ref, init_hbm, o_hbm, table, sem, *, tc):
    i = pl.program_id(0)

    @pl.when(i == 0)
    def _():
        cp = pltpu.make_async_copy(init_hbm, table, sem)
        cp.start()
        cp.wait()

    base = i * tc

    def body(t, c):
        for k in range(TOP_K):
            table[dest_ref[k, t]] = base + t
        return c

    lax.fori_loop(0, tc, body, 0, unroll=2)

    @pl.when(i == pl.num_programs(0) - 1)
    def _():
        cp = pltpu.make_async_copy(table, o_hbm, sem)
        cp.start()
        cp.wait()


def row_tokens(dest, n_rows):
    K, T = dest.shape
    tc = 1024 if T % 1024 == 0 else T
    init = jnp.arange(n_rows, dtype=I32) % T
    return pl.pallas_call(
        functools.partial(_rowtok_kernel, tc=tc),
        grid=(T // tc,),
        in_specs=[pl.BlockSpec((K, tc), lambda i: (0, i), memory_space=pltpu.SMEM),
                  pl.BlockSpec(memory_space=pl.ANY)],
        out_specs=pl.BlockSpec(memory_space=pl.ANY),
        out_shape=jax.ShapeDtypeStruct((n_rows,), I32),
        scratch_shapes=[pltpu.SMEM((n_rows,), I32), pltpu.SemaphoreType.DMA(())],
        compiler_params=_cparams(("arbitrary",)),
        name="row_tokens",
    )(dest, init)


def moe_ffn(x1, mod, g_ffn, w_router, router_bias, w_exp_gate, w_exp_up, w_exp_down,
            w_sh_gate, w_sh_up, w_sh_down, g_final, S):
    T, D = x1.shape
    n_blocks = -(-(T * TOP_K + N_EXPERTS * (DISPATCH_BLOCK - 1)) // DISPATCH_BLOCK)
    h, hb, h3 = ffn_input(x1, g_ffn, mod, S)
    top_i, top_w, pos, counts = router(h, w_router, router_bias)
    dest, block_e, block_valid, next_e = dispatch_plan(top_i, pos, counts, n_blocks)
    row_tok = row_tokens(dest, n_blocks * DISPATCH_BLOCK)
    y = grouped_experts(h3, row_tok, block_e, block_valid, next_e, w_exp_gate, w_exp_up, w_exp_down)
    shared = shared_expert(hb, w_sh_gate, w_sh_up, w_sh_down)
    return combine(dest, y, top_w.T, shared, x1, mod, g_final, S)


def kernel(x, c, positions, w_ada, b_ada, g_mix, w_in, cmp_k_pe, cmp_k_w1, cmp_k_w2, cmp_v_pe, cmp_v_w1,
           cmp_v_w2, conv_w, conv_b, conv_ln_g, conv_ln_b, w_out, g_ffn, w_router, router_bias, w_exp_gate,
           w_exp_up, w_exp_down, w_sh_gate, w_sh_up, w_sh_down, g_final):
    B, S, D = x.shape
    assert w_ada.shape[0] == 1
    x2 = x.reshape(B * S, D)
    mod = ada_mod(c, w_ada[0], b_ada[0])
    cs, sn = rope_table(positions)
    proj = in_projection(x2, g_mix[0], mod, _prep_w_in(w_in[0]), cs, sn, S)
    k_c, v_c = compress_kv(proj, cs, sn, B, S, cmp_k_pe[0], cmp_k_w1[0], cmp_k_w2[0],
                           cmp_v_pe[0], cmp_v_w1[0], cmp_v_w2[0])
    o_attn = nsa_attention(proj, k_c, v_c, B, S)
    o_conv = conformer_conv(proj, conv_w[0], conv_b[0], conv_ln_g[0], conv_ln_b[0], B, S)
    x1 = out_projection(o_attn, o_conv, w_out[0], x2, mod, S)
    out = moe_ffn(x1, mod, g_ffn[0], w_router[0], router_bias[0], w_exp_gate[0], w_exp_up[0],
                  w_exp_down[0], w_sh_gate[0], w_sh_up[0], w_sh_down[0], g_final, S)
    return out.reshape(B, S, D)
```
